```python
import jax, jax.numpy as jnp
from jax import lax
import numpy as np

D_MODEL = 1024
BATCH = 2
SEQ = 16384
DEPTH = 2

CHUNK = 64
N_META = 16
BLOCK_Q = 128
EPS = 1e-6

SB_HEADS = 8
SB_HEAD_DIM = D_MODEL // 16
SB_WIDTH = SB_HEADS * SB_HEAD_DIM
CONV_CH = D_MODEL // 2
CONV_K = 3
IN_COLS = 3 * SB_WIDTH + 3 * CONV_CH

POOL_WINDOWS = (2, 4, 8, 16)
POOL_CH = D_MODEL // len(POOL_WINDOWS)

PEER_HEADS = 8
PEER_NKEYS = 128
PEER_N = PEER_NKEYS * PEER_NKEYS
PEER_TOPK = 16
PEER_HALF = 128
PEER_TOKEN_BLOCK = 128

N_EVEN = (DEPTH + 1) // 2
N_ODD = DEPTH // 2

kernel_name = 'hybrid_stickbreak_conv_pool_peer_encoder'


def rms_norm(x, g):
    xf = x.astype(jnp.float32)
    y = xf * lax.rsqrt(jnp.mean(xf * xf, axis=-1, keepdims=True) + EPS)
    return (y * g.astype(jnp.float32)).astype(x.dtype)


def stick_breaking_attention(q, k, v):
    b, h, lp, dh = q.shape
    nb = lp // BLOCK_Q
    scale = dh ** -0.5
    kf = k.astype(jnp.float32)
    vf = v.astype(jnp.float32)
    qb = q.astype(jnp.float32).reshape(b, h, nb, BLOCK_Q, dh).transpose(2, 0, 1, 3, 4)
    key_pos = jnp.arange(lp)

    def one_block(args):
        q_blk, blk = args
        z = jnp.einsum('bhqd,bhkd->bhqk', q_blk, kf) * scale
        q_pos = blk * BLOCK_Q + jnp.arange(BLOCK_Q)
        causal = key_pos[None, :] < q_pos[:, None]
        log_keep = jnp.where(causal, jax.nn.log_sigmoid(-z), 0.0)
        later = lax.cumsum(log_keep, axis=3, reverse=True) - log_keep
        w = jnp.where(causal, jnp.exp(jax.nn.log_sigmoid(z) + later), 0.0)
        return jnp.einsum('bhqk,bhkd->bhqd', w, vf)

    out = lax.map(one_block, (qb, jnp.arange(nb)))
    return out.transpose(1, 2, 0, 3, 4).reshape(b, h, lp, dh).astype(v.dtype)


def causal_short_conv(u, w):
    lp = u.shape[1]
    up = jnp.pad(u, ((0, 0), (CONV_K - 1, 0), (0, 0)))
    y = w[0] * up[:, 0:lp]
    for i in range(1, CONV_K):
        y = y + w[i] * up[:, i:i + lp]
    return y


def even_layer(x, norm_g, w_in, q_g, k_g, conv_w, w_out):
    b, lp, _ = x.shape
    h = rms_norm(x, norm_g)
    proj = h @ w_in
    cuts = [SB_WIDTH, 2 * SB_WIDTH, 3 * SB_WIDTH, 3 * SB_WIDTH + CONV_CH, 3 * SB_WIDTH + 2 * CONV_CH]
    q, k, v, gate_b, gate_c, u = jnp.split(proj, cuts, axis=-1)
    q = rms_norm(q.reshape(b, lp, SB_HEADS, SB_HEAD_DIM), q_g).transpose(0, 2, 1, 3)
    k = rms_norm(k.reshape(b, lp, SB_HEADS, SB_HEAD_DIM), k_g).transpose(0, 2, 1, 3)
    v = v.reshape(b, lp, SB_HEADS, SB_HEAD_DIM).transpose(0, 2, 1, 3)
    att = stick_breaking_attention(q, k, v).transpose(0, 2, 1, 3).reshape(b, lp, SB_WIDTH)
    conv = gate_b * causal_short_conv(gate_c * u, conv_w)
    return jnp.concatenate([att, conv], axis=-1) @ w_out


def odd_layer(x, norm_g, pool_w, pool_scale):
    b, lp, _ = x.shape
    hf = rms_norm(x, norm_g).astype(jnp.float32)
    csum = jnp.pad(jnp.cumsum(hf, axis=1), ((0, 0), (1, 0), (0, 0)))
    count = jnp.arange(1, lp + 1).astype(jnp.float32)
    diffs = []
    for g, w in enumerate(POOL_WINDOWS):
        sl = slice(g * POOL_CH, (g + 1) * POOL_CH)
        c = csum[:, :, sl]
        lo = jnp.pad(c, ((0, 0), (w - 1, 0), (0, 0)))[:, :lp]
        mean = (c[:, 1:] - lo) / jnp.minimum(count, float(w))[None, :, None]
        diffs.append(mean - hf[:, :, sl])
    d = jnp.stack(diffs, axis=2)
    y = jnp.einsum('blgc,gce->blge', d, pool_w.astype(jnp.float32)).reshape(b, lp, D_MODEL)
    return (y * pool_scale.astype(jnp.float32)).astype(x.dtype)


def peer_ffn(x, norm_g, w_query, sub_keys, expert_u, expert_v):
    b, lp, d = x.shape
    h = rms_norm(x, norm_g).reshape(-1, PEER_TOKEN_BLOCK, d)
    keys = sub_keys.astype(jnp.float32)

    def one_block(hb):
        t = hb.shape[0]
        q = (hb @ w_query).astype(jnp.float32).reshape(t, PEER_HEADS, 2, PEER_HALF)
        s = jnp.einsum('thpc,phkc->thpk', q, keys)
        sv, si = lax.top_k(s, PEER_TOPK)
        cand = (sv[:, :, 0, :, None] + sv[:, :, 1, None, :]).reshape(t, PEER_HEADS, PEER_TOPK * PEER_TOPK)
        cand_idx = (si[:, :, 0, :, None] * PEER_NKEYS + si[:, :, 1, None, :]).reshape(t, PEER_HEADS, PEER_TOPK * PEER_TOPK)
        top_s, top_pos = lax.top_k(cand, PEER_TOPK)
        idx = jnp.take_along_axis(cand_idx, top_pos, axis=-1).reshape(t, PEER_HEADS * PEER_TOPK)
        gate = jax.nn.softmax(top_s, axis=-1).reshape(t, PEER_HEADS * PEER_TOPK)
        u = expert_u[idx]
        act = jax.nn.gelu(jnp.einsum('tnd,td->tn', u, hb).astype(jnp.float32))
        return jnp.einsum('tn,tnd->td', (gate * act).astype(expert_v.dtype), expert_v[idx])

    y = lax.map(one_block, h)
    return y.reshape(b, lp, d).astype(x.dtype)


def setup_inputs(seed: int = 0) -> dict:
    key = jax.random.key(seed)
    ks = jax.random.split(key, 16)
    f32 = jnp.float32
    nrm = lambda k, shape, scale: jax.random.normal(k, shape, f32) * scale
    gain = lambda k, shape: 1.0 + 0.02 * jax.random.normal(k, shape, f32)
    return {
        'x': nrm(ks[0], (BATCH, SEQ, D_MODEL), 1.0),
        'meta_tokens': nrm(ks[1], (N_META, D_MODEL), 1.0),
        'even_norm_g': gain(ks[2], (N_EVEN, D_MODEL)),
        'even_w_in': nrm(ks[3], (N_EVEN, D_MODEL, IN_COLS), D_MODEL ** -0.5),
        'even_q_norm_g': gain(ks[4], (N_EVEN, SB_HEAD_DIM)),
        'even_k_norm_g': gain(ks[5], (N_EVEN, SB_HEAD_DIM)),
        'even_conv_w': nrm(ks[6], (N_EVEN, CONV_K, CONV_CH), CONV_K ** -0.5),
        'even_w_out': nrm(ks[7], (N_EVEN, SB_WIDTH + CONV_CH, D_MODEL), (SB_WIDTH + CONV_CH) ** -0.5),
        'odd_norm_g': gain(ks[8], (N_ODD, D_MODEL)),
        'odd_pool_w': nrm(ks[9], (N_ODD, len(POOL_WINDOWS), POOL_CH, POOL_CH), POOL_CH ** -0.5),
        'odd_pool_scale': gain(ks[10], (N_ODD, D_MODEL)),
        'ffn_norm_g': gain(ks[11], (DEPTH, D_MODEL)),
        'peer_w_query': nrm(ks[12], (DEPTH, D_MODEL, PEER_HEADS * 2 * PEER_HALF), D_MODEL ** -0.5),
        'peer_sub_keys': nrm(ks[13], (DEPTH, 2, PEER_HEADS, PEER_NKEYS, PEER_HALF), PEER_HALF ** -0.5),
        'peer_u': nrm(ks[14], (DEPTH, PEER_N, D_MODEL), D_MODEL ** -0.5),
        'peer_v': nrm(ks[15], (DEPTH, PEER_N, D_MODEL), PEER_HEADS ** -0.5),
    }


def reference(x, meta_tokens, even_norm_g, even_w_in, even_q_norm_g, even_k_norm_g, even_conv_w, even_w_out, odd_norm_g, odd_pool_w, odd_pool_scale, ffn_norm_g, peer_w_query, peer_sub_keys, peer_u, peer_v):
    b, seq, d = x.shape
    total = N_META + seq
    lp = -(-total // BLOCK_Q) * BLOCK_Q
    meta = jnp.broadcast_to(meta_tokens[None].astype(x.dtype), (b, N_META, d))
    h = jnp.concatenate([meta, x], axis=1)
    h = jnp.pad(h, ((0, 0), (0, lp - total), (0, 0)))
    for layer in range(DEPTH):
        i = layer // 2
        if layer % 2 == 0:
            h = h + even_layer(h, even_norm_g[i], even_w_in[i], even_q_norm_g[i], even_k_norm_g[i], even_conv_w[i], even_w_out[i])
        else:
            h = h + odd_layer(h, odd_norm_g[i], odd_pool_w[i], odd_pool_scale[i])
        h = h + peer_ffn(h, ffn_norm_g[layer], peer_w_query[layer], peer_sub_keys[layer], peer_u[layer], peer_v[layer])
    return h[:, N_META:N_META + seq]
```

```python
import functools
import math

import jax
import jax.numpy as jnp
from jax import lax
from jax.experimental import pallas as pl
from jax.experimental.pallas import tpu as pltpu

F32 = jnp.float32
BF16 = jnp.bfloat16

EPS = 1e-6
N_META = 16
SEQ_ALIGN = 128
POOL_WINDOWS = (2, 4, 8, 16)
CONV_K = 3
SB_HEADS = 8
PEER_HEADS = 8
PEER_TOPK = 16

LANES = 128
NEG_INF = float("-inf")
SB_DEAD_LOG = -105.0

VMEM_LIMIT = 48 * 1024 * 1024


def _cparams(sem):
    return pltpu.CompilerParams(dimension_semantics=sem, vmem_limit_bytes=VMEM_LIMIT)


def _split_bf16(x):
    hi = x.astype(BF16)
    lo = (x - hi.astype(F32)).astype(BF16)
    return hi, lo


def _dot(a, b):
    return jnp.dot(a, b, preferred_element_type=F32)


def _rms_rows(x, g):
    ms = jnp.mean(x * x, axis=-1, keepdims=True)
    return x * lax.rsqrt(ms + EPS) * g


def _even_front_kernel(h_ref, g_ref, win_ref, qg_ref, kg_ref, cw_ref, bd_ref,
                       q_ref, kt_ref, v_ref, conv_ref, cbuf, *, tile, width, head_dim):
    i = pl.program_id(1)
    hn = _rms_rows(h_ref[...], g_ref[...])
    proj = _dot(hn.astype(BF16), win_ref[...])
    bd = bd_ref[...]

    def head_norm(t, gain):
        hi, lo = _split_bf16(t * t)
        ssq = _dot(hi, bd) + _dot(lo, bd)
        return t * lax.rsqrt(ssq * (1.0 / head_dim) + EPS) * gain

    q = head_norm(proj[:, 0:width], qg_ref[...]) * (head_dim ** -0.5)
    k = head_norm(proj[:, width:2 * width], kg_ref[...])
    q_ref[...] = q.astype(BF16)
    kt_ref[...] = k.T.astype(BF16)
    v_ref[...] = proj[:, 2 * width:3 * width].astype(BF16)

    gate_b = proj[:, 3 * width:4 * width]
    cu = proj[:, 4 * width:5 * width] * proj[:, 5 * width:6 * width]

    @pl.when(i == 0)
    def _():
        cbuf[0:8, :] = jnp.zeros((8, width), F32)

    cbuf[8:8 + tile, :] = cu
    y = cw_ref[2:3, :] * cu
    for s in range(1, CONV_K):
        y = y + cw_ref[CONV_K - 1 - s:CONV_K - s, :] * cbuf[8 - s:8 - s + tile, :]
    conv_ref[...] = (gate_b * y).astype(BF16)
    cbuf[0:8, :] = cu[tile - 8:tile, :]


def _even_front(h, norm_g, w_in, q_g, k_g, conv_w, tile):
    b, lp, d = h.shape
    width = w_in.shape[1] // 6
    head_dim = q_g.shape[0]
    heads = width // head_dim
    hid = jnp.arange(width) // head_dim
    bd = (hid[:, None] == hid[None, :]).astype(BF16)
    kern = functools.partial(_even_front_kernel, tile=tile, width=width, head_dim=head_dim)
    out_shapes = (
        jax.ShapeDtypeStruct((b, lp, width), BF16),
        jax.ShapeDtypeStruct((b, width, lp), BF16),
        jax.ShapeDtypeStruct((b, lp, width), BF16),
        jax.ShapeDtypeStruct((b, lp, width), BF16),
    )
    row_spec = pl.BlockSpec((None, tile, width), lambda bi, i: (bi, i, 0))
    full = lambda shape: pl.BlockSpec(shape, lambda bi, i: (0,) * len(shape))
    return pl.pallas_call(
        kern,
        grid=(b, lp // tile),
        in_specs=[
            pl.BlockSpec((None, tile, d), lambda bi, i: (bi, i, 0)),
            full((1, d)), full((d, 6 * width)), full((1, width)), full((1, width)),
            full((CONV_K, width)), full((width, width)),
        ],
        out_specs=(row_spec, pl.BlockSpec((None, width, tile), lambda bi, i: (bi, 0, i)), row_spec, row_spec),
        out_shape=out_shapes,
        scratch_shapes=[pltpu.VMEM((tile + 8, width), F32)],
        compiler_params=_cparams(("arbitrary", "arbitrary")),
        name="even_front",
    )(h, norm_g.reshape(1, d), w_in.astype(BF16), jnp.tile(q_g, heads).reshape(1, width),
      jnp.tile(k_g, heads).reshape(1, width), conv_w, bd)


def _stickbreak_kernel(q_ref, kt_ref, v_ref, uj_ref, o_ref, carry_ref, acc_ref, *, blk, head_dim):
    i = pl.program_id(2)
    q = q_ref[...]
    lane = lax.broadcasted_iota(jnp.int32, q.shape, 1)
    zero = jnp.zeros_like(q)
    q_heads = (jnp.where(lane < head_dim, q, zero), jnp.where(lane >= head_dim, q, zero))
    row = lax.broadcasted_iota(jnp.int32, (blk, blk), 0)
    col = lax.broadcasted_iota(jnp.int32, (blk, blk), 1)
    uj = uj_ref[...]

    carry_ref[...] = jnp.zeros(carry_ref.shape, F32)
    acc_ref[...] = jnp.zeros(acc_ref.shape, F32)

    def sweep_block(j):
        start = pl.multiple_of(j * blk, blk)
        kt = kt_ref[:, pl.ds(start, blk)]
        vb = v_ref[pl.ds(start, blk), :]
        causal = (j * blk + col) < (i * blk + row)
        top = None
        for a in range(2):
            z = _dot(q_heads[a], kt)
            l1p = jnp.log(1.0 + jnp.exp(-jnp.abs(z)))
            log_keep = jnp.where(causal, -(jnp.maximum(z, 0.0) + l1p), 0.0)
            log_beta = jnp.minimum(z, 0.0) - l1p
            cum = _dot(log_keep.astype(BF16), uj)
            carry = carry_ref[a]
            later = cum[:, :blk] + carry
            w = jnp.where(causal, jnp.exp(log_beta + later), 0.0)
            acc_ref[a] += _dot(w.astype(BF16), vb)
            carry = carry + cum[:, blk:]
            carry_ref[a] = carry
            m = jnp.max(carry)
            top = m if top is None else jnp.maximum(top, m)
        return top

    top0 = sweep_block(i)

    def cond(state):
        j, top = state
        return jnp.logical_and(j >= 0, top > SB_DEAD_LOG)

    def body(state):
        j, _ = state
        return j - 1, sweep_block(j)

    lax.while_loop(cond, body, (i - 1, top0))
    o_ref[...] = jnp.where(lane < head_dim, acc_ref[0], acc_ref[1]).astype(BF16)


def _stickbreak_attention(q, kt, v, head_dim):
    b, lp, width = q.shape
    blk = SEQ_ALIGN
    pair = 2 * head_dim
    assert pair == LANES and width % pair == 0 and lp % blk == 0
    r = jnp.arange(blk)
    later_mat = (r[:, None] > r[None, :]).astype(BF16)
    uj = jnp.concatenate([later_mat, jnp.ones((blk, blk), BF16)], axis=1)
    kern = functools.partial(_stickbreak_kernel, blk=blk, head_dim=head_dim)
    return pl.pallas_call(
        kern,
        grid=(b, width // pair, lp // blk),
        in_specs=[
            pl.BlockSpec((None, blk, pair), lambda bi, p, i: (bi, i, p)),
            pl.BlockSpec((None, pair, lp), lambda bi, p, i: (bi, p, 0)),
            pl.BlockSpec((None, lp, pair), lambda bi, p, i: (bi, 0, p)),
            pl.BlockSpec((blk, 2 * blk), lambda bi, p, i: (0, 0)),
        ],
        out_specs=pl.BlockSpec((None, blk, pair), lambda bi, p, i: (bi, i, p)),
        out_shape=jax.ShapeDtypeStruct((b, lp, width), BF16),
        scratch_shapes=[pltpu.VMEM((2, blk, blk), F32), pltpu.VMEM((2, blk, pair), F32)],
        compiler_params=_cparams(("arbitrary", "arbitrary", "arbitrary")),
        name="stickbreak_attention",
    )(q, kt, v, uj)


def _even_out_kernel(h_ref, att_ref, conv_ref, wa_ref, wc_ref, o_ref):
    o_ref[...] = h_ref[...] + _dot(att_ref[...], wa_ref[...]) + _dot(conv_ref[...], wc_ref[...])


def _even_out(h2, att2, conv2, w_out, tile):
    n, d = h2.shape
    width = att2.shape[1]
    w = w_out.astype(BF16)
    return pl.pallas_call(
        _even_out_kernel,
        grid=(n // tile,),
        in_specs=[
            pl.BlockSpec((tile, d), lambda i: (i, 0)),
            pl.BlockSpec((tile, width), lambda i: (i, 0)),
            pl.BlockSpec((tile, width), lambda i: (i, 0)),
            pl.BlockSpec((width, d), lambda i: (0, 0)),
            pl.BlockSpec((width, d), lambda i: (0, 0)),
        ],
        out_specs=pl.BlockSpec((tile, d), lambda i: (i, 0)),
        out_shape=jax.ShapeDtypeStruct((n, d), F32),
        compiler_params=_cparams(("arbitrary",)),
        name="even_out",
    )(h2, att2, conv2, w[:width], w[width:])


def _odd_kernel(h_ref, g_ref, pw_ref, ps_ref, o_ref, buf, *, tile, group):
    i = pl.program_id(1)
    halo = max(POOL_WINDOWS)
    x = h_ref[...]
    hn = _rms_rows(x, g_ref[...])

    @pl.when(i == 0)
    def _():
        buf[0:halo, :] = jnp.zeros((halo, hn.shape[1]), F32)

    buf[halo:halo + tile, :] = hn
    pos = i * tile + lax.broadcasted_iota(jnp.int32, (tile, group), 0)
    ys = []
    for gi, w in enumerate(POOL_WINDOWS):
        c0 = gi * group
        cur = hn[:, c0:c0 + group]
        s = cur
        for sft in range(1, w):
            s = s + buf[halo - sft:halo - sft + tile, c0:c0 + group]
        count = jnp.minimum(pos + 1, w).astype(F32)
        diff = s / count - cur
        ys.append(_dot(diff.astype(BF16), pw_ref[gi]))
    y = jnp.concatenate(ys, axis=1)
    o_ref[...] = x + y * ps_ref[...]
    buf[0:halo, :] = hn[tile - halo:tile, :]


def _odd_layer(h, norm_g, pool_w, pool_scale, tile):
    b, lp, d = h.shape
    groups, group, _ = pool_w.shape
    assert groups == len(POOL_WINDOWS) and groups * group == d
    kern = functools.partial(_odd_kernel, tile=tile, group=group)
    return pl.pallas_call(
        kern,
        grid=(b, lp // tile),
        in_specs=[
            pl.BlockSpec((None, tile, d), lambda bi, i: (bi, i, 0)),
            pl.BlockSpec((1, d), lambda bi, i: (0, 0)),
            pl.BlockSpec((groups, group, group), lambda bi, i: (0, 0, 0)),
            pl.BlockSpec((1, d), lambda bi, i: (0, 0)),
        ],
        out_specs=pl.BlockSpec((None, tile, d), lambda bi, i: (bi, i, 0)),
        out_shape=jax.ShapeDtypeStruct((b, lp, d), F32),
        scratch_shapes=[pltpu.VMEM((tile + max(POOL_WINDOWS), d), F32)],
        compiler_params=_cparams(("arbitrary", "arbitrary")),
        name="odd_layer",
    )(h, norm_g.reshape(1, d), pool_w.astype(BF16), pool_scale.reshape(1, d))


def _top16_rows(s):
    rows = s.shape[0]
    row_id = lax.broadcasted_iota(jnp.int32, s.shape, 0).astype(F32)
    rank = jnp.full(s.shape, float(PEER_TOPK), F32)
    vals = []
    for it in range(PEER_TOPK):
        m = jnp.max(s, axis=0, keepdims=True)
        first = jnp.min(jnp.where(s == m, row_id, float(rows)), axis=0, keepdims=True)
        hit = row_id == first
        rank = jnp.where(hit, float(it), rank)
        s = jnp.where(hit, NEG_INF, s)
        vals.append(m)
    return rank, vals


def _pair_partner_counts(v1, v2):
    k = PEER_TOPK
    half = k // 2
    v2_all = jnp.concatenate(v2, axis=0)
    v2_lo = v2_all[0:half]
    sub8 = lax.broadcasted_iota(jnp.int32, (half, LANES), 0)
    sub16 = lax.broadcasted_iota(jnp.int32, (k, LANES), 0)
    tiles = [v1[0] + v2_all]
    flats = [sub16.astype(F32)]
    for a in range(1, half):
        bmax = k // (a + 1) - 1
        tiles.append(jnp.where(sub8 <= bmax, v1[a] + v2_lo, NEG_INF))
        flats.append((sub8 + a * k).astype(F32))
    tiles.append(jnp.concatenate(v1[half:], axis=0) + v2[0])
    flats.append(((sub8 + half) * k).astype(F32))
    sels = [jnp.zeros(t.shape, F32) for t in tiles]

    top = v1[0] + v2[0]
    denom = jnp.zeros((1, LANES), F32)
    big = float(k * k)
    for _ in range(k):
        m = tiles[0].max(axis=0, keepdims=True)
        for t in tiles[1:]:
            m = jnp.maximum(m, t.max(axis=0, keepdims=True))
        first = None
        for t, f in zip(tiles, flats):
            c = jnp.min(jnp.where(t == m, f, big), axis=0, keepdims=True)
            first = c if first is None else jnp.minimum(first, c)
        for n in range(len(tiles)):
            hit = flats[n] == first
            sels[n] = jnp.where(hit, 1.0, sels[n])
            tiles[n] = jnp.where(hit, NEG_INF, tiles[n])
        denom = denom + jnp.exp(m - top)
    counts = [jnp.sum(sels[a], axis=0, keepdims=True) for a in range(half)]
    counts += [sels[half][r:r + 1] for r in range(half)]
    return counts, denom


def _peer_route_kernel(h_ref, g_ref, wqh_ref, wql_ref, kh_ref, kl_ref,
                       hnt_ref, r2_ref, f2_ref, n1_ref, f1_ref, q_scr, *, tile, nkeys):
    hn = _rms_rows(h_ref[...], g_ref[...])
    hnt = hn.T
    hn_hi, hn_lo = _split_bf16(hnt)
    hnt_ref[...] = hn_hi
    wqh = wqh_ref[...]
    q_scr[...] = _dot(wqh, hn_hi) + _dot(wqh, hn_lo) + _dot(wql_ref[...], hn_hi)

    def per_head(hh, _):
        scores = []
        for p in range(2):
            base = pl.multiple_of((hh * 2 + p) * nkeys, nkeys)
            q_hi, q_lo = _split_bf16(q_scr[pl.ds(base, nkeys), :])
            kh = kh_ref[hh * 2 + p]
            scores.append(_dot(kh, q_hi) + _dot(kh, q_lo) + _dot(kl_ref[hh * 2 + p], q_hi))
        for c in range(tile // LANES):
            cs = slice(c * LANES, (c + 1) * LANES)
            s1 = scores[0][:, cs]
            s2 = scores[1][:, cs]
            rank1, v1 = _top16_rows(s1)
            rank2, v2 = _top16_rows(s2)
            counts, denom = _pair_partner_counts(v1, v2)
            n1 = jnp.zeros(s1.shape, F32)
            for a in range(PEER_TOPK):
                n1 = jnp.where(rank1 == float(a), counts[a], n1)
            n1_ref[hh, :, cs] = n1
            f1_ref[hh, :, cs] = jnp.exp(s1 - v1[0])
            r2_ref[hh, :, cs] = rank2.astype(BF16)
            f2_ref[hh, :, cs] = (jnp.exp(s2 - v2[0]) / denom).astype(BF16)
        return 0

    lax.fori_loop(0, PEER_HEADS, per_head, 0)


def _peer_route(h2, norm_g, w_query, sub_keys, tile):
    n, d = h2.shape
    _, heads, nkeys, half = sub_keys.shape
    assert heads == PEER_HEADS and nkeys == half == LANES
    qcols = heads * 2 * half
    wq_hi, wq_lo = _split_bf16(w_query.T)
    keys = sub_keys.transpose(1, 0, 2, 3).reshape(heads * 2, nkeys, half)
    k_hi, k_lo = _split_bf16(keys)
    kern = functools.partial(_peer_route_kernel, tile=tile, nkeys=nkeys)
    tab = lambda dt: jax.ShapeDtypeStruct((heads, nkeys, n), dt)
    tab_spec = pl.BlockSpec((heads, nkeys, tile), lambda i: (0, 0, i))
    const = lambda shape: pl.BlockSpec(shape, lambda i: (0,) * len(shape))
    return pl.pallas_call(
        kern,
        grid=(n // tile,),
        in_specs=[
            pl.BlockSpec((tile, d), lambda i: (i, 0)),
            const((1, d)), const((qcols, d)), const((qcols, d)),
            const((heads * 2, nkeys, half)), const((heads * 2, nkeys, half)),
        ],
        out_specs=(pl.BlockSpec((d, tile), lambda i: (0, i)), tab_spec, tab_spec, tab_spec, tab_spec),
        out_shape=(jax.ShapeDtypeStruct((d, n), BF16), tab(BF16), tab(BF16), tab(F32), tab(F32)),
        scratch_shapes=[pltpu.VMEM((qcols, tile), F32)],
        compiler_params=_cparams(("arbitrary",)),
        name="peer_route",
    )(h2, norm_g.reshape(1, d), wq_hi, wq_lo, k_hi, k_lo)


def _gelu_tanh(x):
    return 0.5 * x * (1.0 + jnp.tanh(math.sqrt(2.0 / math.pi) * (x + 0.044715 * (x * x * x))))


def _peer_expert_kernel(hnt_ref, u_ref, vt_ref, r2_ref, f2_ref, n1_ref, f1_ref, h_ref,
                        o_ref, acc_ref, ga_ref, *, tile, nkeys, rows_per_step):
    e = pl.program_id(1)

    @pl.when(e == 0)
    def _():
        acc_ref[...] = jnp.zeros(acc_ref.shape, F32)

    act = _gelu_tanh(_dot(u_ref[...], hnt_ref[...]))
    sub = 16
    for r in range(rows_per_step):
        bcast = []
        for hh in range(PEER_HEADS):
            n1 = jnp.broadcast_to(n1_ref[hh, r:r + 1, :], (sub, tile)).astype(BF16)
            f1 = jnp.broadcast_to(f1_ref[hh, r:r + 1, :], (sub, tile)).astype(BF16)
            bcast.append((n1, f1))
        for g in range(nkeys // sub):
            rs = slice(g * sub, (g + 1) * sub)
            gate = jnp.zeros((sub, tile), BF16)
            for hh in range(PEER_HEADS):
                n1, f1 = bcast[hh]
                gate = gate + jnp.where(r2_ref[hh, rs, :] < n1, f2_ref[hh, rs, :], jnp.zeros((), BF16)) * f1
            o0 = r * nkeys + g * sub
            ga_ref[o0:o0 + sub, :] = act[o0:o0 + sub, :].astype(BF16) * gate
    acc_ref[...] += _dot(vt_ref[...], ga_ref[...])

    @pl.when(e == pl.num_programs(1) - 1)
    def _():
        o_ref[...] = h_ref[...] + acc_ref[...].T


def _peer_experts(h2, hnt, r2, f2, n1, f1, expert_u, expert_v, tile, rows_per_step):
    n, d = h2.shape
    heads, nkeys, _ = r2.shape
    n_exp = expert_u.shape[0]
    te = rows_per_step * nkeys
    assert n_exp == nkeys * nkeys and n_exp % te == 0 and rows_per_step % 8 == 0
    u = expert_u.astype(BF16)
    vt = expert_v.T.astype(BF16)
    kern = functools.partial(_peer_expert_kernel, tile=tile, nkeys=nkeys, rows_per_step=rows_per_step)
    tab2 = pl.BlockSpec((heads, nkeys, tile), lambda t, e: (0, 0, t))
    tab1 = pl.BlockSpec((heads, rows_per_step, tile), lambda t, e: (0, e, t))
    return pl.pallas_call(
        kern,
        grid=(n // tile, n_exp // te),
        in_specs=[
            pl.BlockSpec((d, tile), lambda t, e: (0, t)),
            pl.BlockSpec((te, d), lambda t, e: (e, 0)),
            pl.BlockSpec((d, te), lambda t, e: (0, e)),
            tab2, tab2, tab1, tab1,
            pl.BlockSpec((tile, d), lambda t, e: (t, 0)),
        ],
        out_specs=pl.BlockSpec((tile, d), lambda t, e: (t, 0)),
        out_shape=jax.ShapeDtypeStruct((n, d), F32),
        scratch_shapes=[pltpu.VMEM((d, tile), F32), pltpu.VMEM((te, tile), BF16)],
        compiler_params=_cparams(("arbitrary", "arbitrary")),
        name="peer_experts",
    )(hnt, u, vt, r2, f2, n1, f1, h2)


def _peer_ffn(h2, norm_g, w_query, sub_keys, expert_u, expert_v, route_tile, expert_tile):
    hnt, r2, f2, n1, f1 = _peer_route(h2, norm_g, w_query, sub_keys, route_tile)
    return _peer_experts(h2, hnt, r2, f2, n1, f1, expert_u, expert_v, expert_tile, rows_per_step=8)


def _pick_tile(n, candidates):
    for c in candidates:
        if n % c == 0:
            return c
    raise ValueError(f"no tile in {candidates} divides {n}")


def kernel(x, meta_tokens, even_norm_g, even_w_in, even_q_norm_g, even_k_norm_g, even_conv_w, even_w_out,
           odd_norm_g, odd_pool_w, odd_pool_scale, ffn_norm_g, peer_w_query, peer_sub_keys, peer_u, peer_v):
    b, seq, d = x.shape
    total = N_META + seq
    lp = -(-total // SEQ_ALIGN) * SEQ_ALIGN
    meta = jnp.broadcast_to(meta_tokens[None].astype(x.dtype), (b, N_META, d))
    h = jnp.concatenate([meta, x], axis=1)
    h = jnp.pad(h, ((0, 0), (0, lp - total), (0, 0)))
    n = b * lp
    seq_tile = _pick_tile(lp, (384, 256, 128))
    flat_tile = _pick_tile(n, (768, 512, 384, 256, 128))
    route_tile = _pick_tile(n, (256, 128))
    depth = ffn_norm_g.shape[0]
    head_dim = even_q_norm_g.shape[1]
    for layer in range(depth):
        i = layer // 2
        if layer % 2 == 0:
            q, kt, v, conv = _even_front(h, even_norm_g[i], even_w_in[i], even_q_norm_g[i], even_k_norm_g[i],
                                         even_conv_w[i], seq_tile)
            att = _stickbreak_attention(q, kt, v, head_dim)
            width = att.shape[-1]
            h2 = _even_out(h.reshape(n, d), att.reshape(n, width), conv.reshape(n, width), even_w_out[i], flat_tile)
        else:
            h2 = _odd_layer(h, odd_norm_g[i], odd_pool_w[i], odd_pool_scale[i], seq_tile).reshape(n, d)
        h2 = _peer_ffn(h2, ffn_norm_g[layer], peer_w_query[layer], peer_sub_keys[layer], peer_u[layer], peer_v[layer],
                       route_tile, flat_tile)
        h = h2.reshape(b, lp, d)
    return h[:, N_META:N_META + seq]
```

```python
import functools
import math

import jax
import jax.numpy as jnp
from jax import lax
from jax.experimental import pallas as pl
from jax.experimental.pallas import tpu as pltpu

F32 = jnp.float32
BF16 = jnp.bfloat16

EPS = 1e-6
N_META = 16
SEQ_ALIGN = 128
POOL_WINDOWS = (2, 4, 8, 16)
CONV_K = 3
SB_HEADS = 8
PEER_HEADS = 8
PEER_TOPK = 16

LANES = 128
NEG_INF = float("-inf")
SB_DEAD_LOG = -105.0

VMEM_LIMIT = 48 * 1024 * 1024


def _cparams(sem):
    return pltpu.CompilerParams(dimension_semantics=sem, vmem_limit_bytes=VMEM_LIMIT)


def _split_bf16(x):
    hi = x.astype(BF16)
    lo = (x - hi.astype(F32)).astype(BF16)
    return hi, lo


def _dot(a, b):
    return jnp.dot(a, b, preferred_element_type=F32)


def _rms_rows(x, g):
    ms = jnp.mean(x * x, axis=-1, keepdims=True)
    return x * lax.rsqrt(ms + EPS) * g


def _even_front_kernel(h_ref, g_ref, win_ref, qg_ref, kg_ref, cw_ref, bd_ref,
                       q_ref, kt_ref, v_ref, conv_ref, cbuf, *, tile, width, head_dim):
    i = pl.program_id(1)
    hn = _rms_rows(h_ref[...], g_ref[...])
    proj = _dot(hn.astype(BF16), win_ref[...])
    bd = bd_ref[...]

    def head_norm(t, gain):
        hi, lo = _split_bf16(t * t)
        ssq = _dot(hi, bd) + _dot(lo, bd)
        return t * lax.rsqrt(ssq * (1.0 / head_dim) + EPS) * gain

    q = head_norm(proj[:, 0:width], qg_ref[...]) * (head_dim ** -0.5)
    k = head_norm(proj[:, width:2 * width], kg_ref[...])
    q_ref[...] = q.astype(BF16)
    kt_ref[...] = k.T.astype(BF16)
    v_ref[...] = proj[:, 2 * width:3 * width].astype(BF16)

    gate_b = proj[:, 3 * width:4 * width]
    cu = proj[:, 4 * width:5 * width] * proj[:, 5 * width:6 * width]

    @pl.when(i == 0)
    def _():
        cbuf[0:8, :] = jnp.zeros((8, width), F32)

    cbuf[8:8 + tile, :] = cu
    y = cw_ref[2:3, :] * cu
    for s in range(1, CONV_K):
        y = y + cw_ref[CONV_K - 1 - s:CONV_K - s, :] * cbuf[8 - s:8 - s + tile, :]
    conv_ref[...] = (gate_b * y).astype(BF16)
    cbuf[0:8, :] = cu[tile - 8:tile, :]


def _even_front(h, norm_g, w_in, q_g, k_g, conv_w, tile):
    b, lp, d = h.shape
    width = w_in.shape[1] // 6
    head_dim = q_g.shape[0]
    heads = width // head_dim
    hid = jnp.arange(width) // head_dim
    bd = (hid[:, None] == hid[None, :]).astype(BF16)
    kern = functools.partial(_even_front_kernel, tile=tile, width=width, head_dim=head_dim)
    out_shapes = (
        jax.ShapeDtypeStruct((b, lp, width), BF16),
        jax.ShapeDtypeStruct((b, width, lp), BF16),
        jax.ShapeDtypeStruct((b, lp, width), BF16),
        jax.ShapeDtypeStruct((b, lp, width), BF16),
    )
    row_spec = pl.BlockSpec((None, tile, width), lambda bi, i: (bi, i, 0))
    full = lambda shape: pl.BlockSpec(shape, lambda bi, i: (0,) * len(shape))
    return pl.pallas_call(
        kern,
        grid=(b, lp // tile),
        in_specs=[
            pl.BlockSpec((None, tile, d), lambda bi, i: (bi, i, 0)),
            full((1, d)), full((d, 6 * width)), full((1, width)), full((1, width)),
            full((CONV_K, width)), full((width, width)),
        ],
        out_specs=(row_spec, pl.BlockSpec((None, width, tile), lambda bi, i: (bi, 0, i)), row_spec, row_spec),
        out_shape=out_shapes,
        scratch_shapes=[pltpu.VMEM((tile + 8, width), F32)],
        compiler_params=_cparams(("arbitrary", "arbitrary")),
        name="even_front",
    )(h, norm_g.reshape(1, d), w_in.astype(BF16), jnp.tile(q_g, heads).reshape(1, width),
      jnp.tile(k_g, heads).reshape(1, width), conv_w, bd)


def _stickbreak_kernel(q_ref, kt_ref, v_ref, uj_ref, o_ref, carry_ref, acc_ref, *, blk, head_dim, pairs):
    i = pl.program_id(1)
    pair = 2 * head_dim
    lane = lax.broadcasted_iota(jnp.int32, (blk, pair), 1)
    row = lax.broadcasted_iota(jnp.int32, (blk, blk), 0)
    col = lax.broadcasted_iota(jnp.int32, (blk, blk), 1)
    uj = uj_ref[...]

    carry_ref[...] = jnp.zeros(carry_ref.shape, F32)
    acc_ref[...] = jnp.zeros(acc_ref.shape, F32)

    def sweep_block(j):
        start = pl.multiple_of(j * blk, blk)
        causal = (j * blk + col) < (i * blk + row)
        heads = range(2 * pairs)
        zs = []
        for p in range(pairs):
            ps = slice(p * pair, (p + 1) * pair)
            q = q_ref[:, ps]
            kt = kt_ref[ps, pl.ds(start, blk)]
            zero = jnp.zeros_like(q)
            zs.append(_dot(jnp.where(lane < head_dim, q, zero), kt))
            zs.append(_dot(jnp.where(lane >= head_dim, q, zero), kt))
        cums, log_betas = [], []
        for hd in heads:
            z = zs[hd]
            l1p = jnp.log(1.0 + jnp.exp(-jnp.abs(z)))
            log_keep = jnp.where(causal, -(jnp.maximum(z, 0.0) + l1p), 0.0)
            log_betas.append(jnp.minimum(z, 0.0) - l1p)
            cums.append(_dot(log_keep.astype(BF16), uj))
        top = None
        for hd in heads:
            p = hd // 2
            vb = v_ref[pl.ds(start, blk), p * pair:(p + 1) * pair]
            carry = carry_ref[hd]
            later = cums[hd][:, :blk] + carry
            w = jnp.where(causal, jnp.exp(log_betas[hd] + later), 0.0)
            acc_ref[hd] += _dot(w.astype(BF16), vb)
            carry = carry + cums[hd][:, blk:]
            carry_ref[hd] = carry
            m = jnp.max(carry, axis=0, keepdims=True)
            top = m if top is None else jnp.maximum(top, m)
        return jnp.max(top)

    top0 = sweep_block(i)

    def cond(state):
        j, top = state
        return jnp.logical_and(j >= 0, top > SB_DEAD_LOG)

    def body(state):
        j, _ = state
        return j - 1, sweep_block(j)

    lax.while_loop(cond, body, (i - 1, top0))
    for p in range(pairs):
        o_ref[:, p * pair:(p + 1) * pair] = jnp.where(lane < head_dim, acc_ref[2 * p], acc_ref[2 * p + 1]).astype(BF16)


def _stickbreak_attention(q, kt, v, head_dim):
    b, lp, width = q.shape
    blk = SEQ_ALIGN
    pair = 2 * head_dim
    assert pair == LANES and width % pair == 0 and lp % blk == 0
    pairs = width // pair
    r = jnp.arange(blk)
    later_mat = (r[:, None] > r[None, :]).astype(BF16)
    uj = jnp.concatenate([later_mat, jnp.ones((blk, blk), BF16)], axis=1)
    kern = functools.partial(_stickbreak_kernel, blk=blk, head_dim=head_dim, pairs=pairs)
    resident = pl.Buffered(1)
    return pl.pallas_call(
        kern,
        grid=(b, lp // blk),
        in_specs=[
            pl.BlockSpec((None, blk, width), lambda bi, i: (bi, i, 0)),
            pl.BlockSpec((None, width, lp), lambda bi, i: (bi, 0, 0), pipeline_mode=resident),
            pl.BlockSpec((None, lp, width), lambda bi, i: (bi, 0, 0), pipeline_mode=resident),
            pl.BlockSpec((blk, 2 * blk), lambda bi, i: (0, 0)),
        ],
        out_specs=pl.BlockSpec((None, blk, width), lambda bi, i: (bi, i, 0)),
        out_shape=jax.ShapeDtypeStruct((b, lp, width), BF16),
        scratch_shapes=[pltpu.VMEM((2 * pairs, blk, blk), F32), pltpu.VMEM((2 * pairs, blk, pair), F32)],
        compiler_params=_cparams(("arbitrary", "arbitrary")),
        name="stickbreak_attention",
    )(q, kt, v, uj)


def _even_out_kernel(h_ref, att_ref, conv_ref, wa_ref, wc_ref, o_ref):
    o_ref[...] = h_ref[...] + _dot(att_ref[...], wa_ref[...]) + _dot(conv_ref[...], wc_ref[...])


def _even_out(h2, att2, conv2, w_out, tile):
    n, d = h2.shape
    width = att2.shape[1]
    w = w_out.astype(BF16)
    return pl.pallas_call(
        _even_out_kernel,
        grid=(n // tile,),
        in_specs=[
            pl.BlockSpec((tile, d), lambda i: (i, 0)),
            pl.BlockSpec((tile, width), lambda i: (i, 0)),
            pl.BlockSpec((tile, width), lambda i: (i, 0)),
            pl.BlockSpec((width, d), lambda i: (0, 0)),
            pl.BlockSpec((width, d), lambda i: (0, 0)),
        ],
        out_specs=pl.BlockSpec((tile, d), lambda i: (i, 0)),
        out_shape=jax.ShapeDtypeStruct((n, d), F32),
        compiler_params=_cparams(("arbitrary",)),
        name="even_out",
    )(h2, att2, conv2, w[:width], w[width:])


def _odd_kernel(h_ref, g_ref, pw_ref, ps_ref, o_ref, buf, *, tile, group):
    i = pl.program_id(1)
    halo = max(POOL_WINDOWS)
    x = h_ref[...]
    hn = _rms_rows(x, g_ref[...])

    @pl.when(i == 0)
    def _():
        buf[0:halo, :] = jnp.zeros((halo, hn.shape[1]), F32)

    buf[halo:halo + tile, :] = hn
    pos = i * tile + lax.broadcasted_iota(jnp.int32, (tile, group), 0)
    ys = []
    for gi, w in enumerate(POOL_WINDOWS):
        c0 = gi * group
        cur = hn[:, c0:c0 + group]
        s = cur
        for sft in range(1, w):
            s = s + buf[halo - sft:halo - sft + tile, c0:c0 + group]
        count = jnp.minimum(pos + 1, w).astype(F32)
        diff = s / count - cur
        ys.append(_dot(diff.astype(BF16), pw_ref[gi]))
    y = jnp.concatenate(ys, axis=1)
    o_ref[...] = x + y * ps_ref[...]
    buf[0:halo, :] = hn[tile - halo:tile, :]


def _odd_layer(h, norm_g, pool_w, pool_scale, tile):
    b, lp, d = h.shape
    groups, group, _ = pool_w.shape
    assert groups == len(POOL_WINDOWS) and groups * group == d
    kern = functools.partial(_odd_kernel, tile=tile, group=group)
    return pl.pallas_call(
        kern,
        grid=(b, lp // tile),
        in_specs=[
            pl.BlockSpec((None, tile, d), lambda bi, i: (bi, i, 0)),
            pl.BlockSpec((1, d), lambda bi, i: (0, 0)),
            pl.BlockSpec((groups, group, group), lambda bi, i: (0, 0, 0)),
            pl.BlockSpec((1, d), lambda bi, i: (0, 0)),
        ],
        out_specs=pl.BlockSpec((None, tile, d), lambda bi, i: (bi, i, 0)),
        out_shape=jax.ShapeDtypeStruct((b, lp, d), F32),
        scratch_shapes=[pltpu.VMEM((tile + max(POOL_WINDOWS), d), F32)],
        compiler_params=_cparams(("arbitrary", "arbitrary")),
        name="odd_layer",
    )(h, norm_g.reshape(1, d), pool_w.astype(BF16), pool_scale.reshape(1, d))


def _top16_rows(s, exact_ties):
    rows = s.shape[0]
    row_id = lax.broadcasted_iota(jnp.int32, s.shape, 0).astype(F32) if exact_ties else None
    rank = jnp.full(s.shape, float(PEER_TOPK), F32)
    vals = []
    for it in range(PEER_TOPK):
        m = jnp.max(s, axis=0, keepdims=True)
        hit = s == m
        if exact_ties:
            first = jnp.min(jnp.where(hit, row_id, float(rows)), axis=0, keepdims=True)
            hit = row_id == first
        rank = jnp.where(hit, float(it), rank)
        s = jnp.where(hit, NEG_INF, s)
        vals.append(m)
    return rank, vals


def _pair_partner_counts(v1, v2, exact_ties):
    k = PEER_TOPK
    half = k // 2
    v2_all = jnp.concatenate(v2, axis=0)
    v2_lo = v2_all[0:half]
    sub8 = lax.broadcasted_iota(jnp.int32, (half, LANES), 0)
    tiles = [v1[0] + v2_all]
    masked = [0]
    for a in range(1, half):
        bmax = k // (a + 1) - 1
        tiles.append(jnp.where(sub8 <= bmax, v1[a] + v2_lo, NEG_INF))
        masked.append(half - 1 - bmax)
    tiles.append(jnp.concatenate(v1[half:], axis=0) + v2[0])
    if exact_ties:
        sub16 = lax.broadcasted_iota(jnp.int32, (k, LANES), 0)
        flats = [sub16.astype(F32)] + [(sub8 + a * k).astype(F32) for a in range(1, half)]
        flats.append(((sub8 + half) * k).astype(F32))

    top = v1[0] + v2[0]
    denom = jnp.zeros((1, LANES), F32)
    big = float(k * k)
    for _ in range(k):
        m = tiles[0].max(axis=0, keepdims=True)
        for t in tiles[1:]:
            m = jnp.maximum(m, t.max(axis=0, keepdims=True))
        if exact_ties:
            first = None
            for t, f in zip(tiles, flats):
                c = jnp.min(jnp.where(t == m, f, big), axis=0, keepdims=True)
                first = c if first is None else jnp.minimum(first, c)
            tiles = [jnp.where(f == first, NEG_INF, t) for t, f in zip(tiles, flats)]
        else:
            tiles = [jnp.where(t == m, NEG_INF, t) for t in tiles]
        denom = denom + jnp.exp(m - top)
    gone = [jnp.where(t == NEG_INF, 1.0, 0.0) for t in tiles]
    counts = [jnp.sum(gone[a], axis=0, keepdims=True) - float(masked[a]) for a in range(half)]
    counts += [gone[half][r:r + 1] for r in range(half)]
    return counts, denom


def _route_chunk(s1, s2, exact_ties):
    k = float(PEER_TOPK)
    rank1, v1 = _top16_rows(s1, exact_ties)
    rank2, v2 = _top16_rows(s2, exact_ties)
    counts, denom = _pair_partner_counts(v1, v2, exact_ties)
    n1 = jnp.zeros(s1.shape, F32)
    for a in range(PEER_TOPK):
        n1 = jnp.where(rank1 == float(a), counts[a], n1)
    f1 = jnp.exp(s1 - v1[0])
    f2 = jnp.exp(s2 - v2[0]) / denom
    if exact_ties:
        bad = jnp.zeros((1, LANES), F32)
    else:
        picked1 = jnp.sum(jnp.where(rank1 < k, 1.0, 0.0), axis=0, keepdims=True)
        picked2 = jnp.sum(jnp.where(rank2 < k, 1.0, 0.0), axis=0, keepdims=True)
        pairs = counts[0]
        for c in counts[1:]:
            pairs = pairs + c
        ok = jnp.logical_and(jnp.logical_and(picked1 == k, picked2 == k), pairs == k)
        bad = jnp.where(ok, 0.0, 1.0)
    return n1, f1, rank2, f2, bad


def _peer_route_kernel(h_ref, g_ref, kwh_ref, kwl_ref,
                       hnt_ref, r2_ref, f2_ref, n1_ref, f1_ref, hi_scr, lo_scr, s_scr, *, tile, nkeys):
    hn = _rms_rows(h_ref[...], g_ref[...])
    hn_hi, hn_lo = _split_bf16(hn.T)
    hnt_ref[...] = hn_hi
    hi_scr[...] = hn_hi
    lo_scr[...] = hn_lo
    chunks = [slice(c * LANES, (c + 1) * LANES) for c in range(tile // LANES)]

    def head_scores(hh, slot):
        rows = pl.ds(pl.multiple_of(hh * 2 * nkeys, 2 * nkeys), 2 * nkeys)
        kwh = kwh_ref[rows, :]
        hi = hi_scr[...]
        s_scr[slot] = _dot(kwh, hi) + _dot(kwh, lo_scr[...]) + _dot(kwl_ref[rows, :], hi)

    head_scores(0, 0)

    def per_head(hh, _):
        slot = lax.rem(hh, 2)
        s1 = s_scr[slot, 0:nkeys, :]
        s2 = s_scr[slot, nkeys:2 * nkeys, :]
        head_scores(jnp.minimum(hh + 1, PEER_HEADS - 1), 1 - slot)

        def write(cs, n1, f1, rank2, f2):
            n1_ref[hh, :, cs] = n1
            f1_ref[hh, :, cs] = f1
            r2_ref[hh, :, cs] = rank2.astype(BF16)
            f2_ref[hh, :, cs] = f2.astype(BF16)

        bad = None
        for cs in chunks:
            n1, f1, rank2, f2, bad_c = _route_chunk(s1[:, cs], s2[:, cs], exact_ties=False)
            write(cs, n1, f1, rank2, f2)
            bad = bad_c if bad is None else jnp.maximum(bad, bad_c)

        @pl.when(jnp.max(bad) > 0.0)
        def _():
            for cs in chunks:
                write(cs, *_route_chunk(s1[:, cs], s2[:, cs], exact_ties=True)[:4])
        return 0

    lax.fori_loop(0, PEER_HEADS, per_head, 0)


def _split3_bf16(x):
    hi = x.astype(BF16)
    r1 = x - hi.astype(F32)
    mid = r1.astype(BF16)
    lo = (r1 - mid.astype(F32)).astype(BF16)
    return hi, mid, lo


def _fold_keys_kernel(k_ref, wq_ref, o_ref):
    kh, km, kl = _split3_bf16(k_ref[...])
    wh, wm, wl = _split3_bf16(wq_ref[...])
    o_ref[...] = (_dot(kh, wh) + (_dot(kh, wm) + _dot(km, wh))
                  + (_dot(km, wm) + _dot(kh, wl) + _dot(kl, wh)))


def _fold_keys(w_query, sub_keys):
    d = w_query.shape[0]
    _, heads, nkeys, half = sub_keys.shape
    keys = sub_keys.transpose(1, 0, 2, 3).reshape(heads * 2, nkeys, half)
    return pl.pallas_call(
        _fold_keys_kernel,
        grid=(heads * 2,),
        in_specs=[pl.BlockSpec((None, nkeys, half), lambda r: (r, 0, 0)),
                  pl.BlockSpec((half, d), lambda r: (r, 0))],
        out_specs=pl.BlockSpec((nkeys, d), lambda r: (r, 0)),
        out_shape=jax.ShapeDtypeStruct((heads * 2 * nkeys, d), F32),
        compiler_params=_cparams(("arbitrary",)),
        name="peer_fold_keys",
    )(keys, w_query.T)


def _peer_route(h2, norm_g, w_query, sub_keys, tile):
    n, d = h2.shape
    _, heads, nkeys, half = sub_keys.shape
    assert heads == PEER_HEADS and nkeys == half == LANES
    srows = heads * 2 * nkeys
    kw_hi, kw_lo = _split_bf16(_fold_keys(w_query, sub_keys))
    kern = functools.partial(_peer_route_kernel, tile=tile, nkeys=nkeys)
    tab = lambda dt: jax.ShapeDtypeStruct((heads, nkeys, n), dt)
    tab_spec = pl.BlockSpec((heads, nkeys, tile), lambda i: (0, 0, i))
    const = lambda shape: pl.BlockSpec(shape, lambda i: (0,) * len(shape))
    return pl.pallas_call(
        kern,
        grid=(n // tile,),
        in_specs=[
            pl.BlockSpec((tile, d), lambda i: (i, 0)),
            const((1, d)), const((srows, d)), const((srows, d)),
        ],
        out_specs=(pl.BlockSpec((d, tile), lambda i: (0, i)), tab_spec, tab_spec, tab_spec, tab_spec),
        out_shape=(jax.ShapeDtypeStruct((d, n), BF16), tab(BF16), tab(BF16), tab(F32), tab(F32)),
        scratch_shapes=[pltpu.VMEM((d, tile), BF16), pltpu.VMEM((d, tile), BF16),
                        pltpu.VMEM((2, 2 * nkeys, tile), F32)],
        compiler_params=_cparams(("arbitrary",)),
        name="peer_route",
    )(h2, norm_g.reshape(1, d), kw_hi, kw_lo)


def _gelu_times_gate(a, gate):
    k = -2.0 * math.sqrt(2.0 / math.pi) * math.log2(math.e)
    u = a * ((a * a) * (k * 0.044715) + k)
    return a.astype(BF16) * gate / (1.0 + jnp.exp2(u.astype(BF16)))


def _peer_expert_kernel(hnt_ref, u_ref, vt_ref, r2_ref, f2_ref, n1_ref, f1_ref, h_ref,
                        o_ref, acc_ref, ga_ref, *, tile, nkeys, rows_per_step):
    e = pl.program_id(1)

    @pl.when(e == 0)
    def _():
        acc_ref[...] = jnp.zeros(acc_ref.shape, F32)

    pre = _dot(u_ref[...], hnt_ref[...])
    sub = 16
    for r in range(rows_per_step):
        bcast = []
        for hh in range(PEER_HEADS):
            n1 = jnp.broadcast_to(n1_ref[hh, r:r + 1, :], (sub, tile)).astype(BF16)
            f1 = jnp.broadcast_to(f1_ref[hh, r:r + 1, :], (sub, tile)).astype(BF16)
            bcast.append((n1, f1))
        for g in range(nkeys // sub):
            rs = slice(g * sub, (g + 1) * sub)
            gate = jnp.zeros((sub, tile), BF16)
            for hh in range(PEER_HEADS):
                n1, f1 = bcast[hh]
                gate = gate + jnp.where(r2_ref[hh, rs, :] < n1, f2_ref[hh, rs, :], jnp.zeros((), BF16)) * f1
            o0 = r * nkeys + g * sub
            ga_ref[o0:o0 + sub, :] = _gelu_times_gate(pre[o0:o0 + sub, :], gate)
    acc_ref[...] += _dot(vt_ref[...], ga_ref[...])

    @pl.when(e == pl.num_programs(1) - 1)
    def _():
        o_ref[...] = h_ref[...] + acc_ref[...].T


def _peer_experts(h2, hnt, r2, f2, n1, f1, expert_u, expert_v, tile, rows_per_step):
    n, d = h2.shape
    heads, nkeys, _ = r2.shape
    n_exp = expert_u.shape[0]
    te = rows_per_step * nkeys
    assert n_exp == nkeys * nkeys and n_exp % te == 0 and rows_per_step % 8 == 0
    u = expert_u.astype(BF16)
    vt = expert_v.T.astype(BF16)
    kern = functools.partial(_peer_expert_kernel, tile=tile, nkeys=nkeys, rows_per_step=rows_per_step)
    tab2 = pl.BlockSpec((heads, nkeys, tile), lambda t, e: (0, 0, t))
    tab1 = pl.BlockSpec((heads, rows_per_step, tile), lambda t, e: (0, e, t))
    return pl.pallas_call(
        kern,
        grid=(n // tile, n_exp // te),
        in_specs=[
            pl.BlockSpec((d, tile), lambda t, e: (0, t)),
            pl.BlockSpec((te, d), lambda t, e: (e, 0)),
            pl.BlockSpec((d, te), lambda t, e: (0, e)),
            tab2, tab2, tab1, tab1,
            pl.BlockSpec((tile, d), lambda t, e: (t, 0)),
        ],
        out_specs=pl.BlockSpec((tile, d), lambda t, e: (t, 0)),
        out_shape=jax.ShapeDtypeStruct((n, d), F32),
        scratch_shapes=[pltpu.VMEM((d, tile), F32), pltpu.VMEM((te, tile), BF16)],
        compiler_params=_cparams(("arbitrary", "arbitrary")),
        name="peer_experts",
    )(hnt, u, vt, r2, f2, n1, f1, h2)


def _peer_ffn(h2, norm_g, w_query, sub_keys, expert_u, expert_v, route_tile, expert_tile):
    hnt, r2, f2, n1, f1 = _peer_route(h2, norm_g, w_query, sub_keys, route_tile)
    return _peer_experts(h2, hnt, r2, f2, n1, f1, expert_u, expert_v, expert_tile, rows_per_step=8)


def _pick_tile(n, candidates):
    for c in candidates:
        if n % c == 0:
            return c
    raise ValueError(f"no tile in {candidates} divides {n}")


def kernel(x, meta_tokens, even_norm_g, even_w_in, even_q_norm_g, even_k_norm_g, even_conv_w, even_w_out,
           odd_norm_g, odd_pool_w, odd_pool_scale, ffn_norm_g, peer_w_query, peer_sub_keys, peer_u, peer_v):
    b, seq, d = x.shape
    total = N_META + seq
    lp = -(-total // SEQ_ALIGN) * SEQ_ALIGN
    meta = jnp.broadcast_to(meta_tokens[None].astype(x.dtype), (b, N_META, d))
    h = jnp.concatenate([meta, x], axis=1)
    h = jnp.pad(h, ((0, 0), (0, lp - total), (0, 0)))
    n = b * lp
    seq_tile = _pick_tile(lp, (384, 256, 128))
    flat_tile = _pick_tile(n, (768, 512, 384, 256, 128))
    route_tile = _pick_tile(n, (256, 128))
    depth = ffn_norm_g.shape[0]
    head_dim = even_q_norm_g.shape[1]
    for layer in range(depth):
        i = layer // 2
        if layer % 2 == 0:
            q, kt, v, conv = _even_front(h, even_norm_g[i], even_w_in[i], even_q_norm_g[i], even_k_norm_g[i],
                                         even_conv_w[i], seq_tile)
            att = _stickbreak_attention(q, kt, v, head_dim)
            width = att.shape[-1]
            h2 = _even_out(h.reshape(n, d), att.reshape(n, width), conv.reshape(n, width), even_w_out[i], flat_tile)
        else:
            h2 = _odd_layer(h, odd_norm_g[i], odd_pool_w[i], odd_pool_scale[i], seq_tile).reshape(n, d)
        h2 = _peer_ffn(h2, ffn_norm_g[layer], peer_w_query[layer], peer_sub_keys[layer], peer_u[layer], peer_v[layer],
                       route_tile, flat_tile)
        h = h2.reshape(b, lp, d)
    return h[:, N_META:N_META + seq]
```

```python
import functools
import math

import jax
import jax.numpy as jnp
from jax import lax
from jax.experimental import pallas as pl
from jax.experimental.pallas import tpu as pltpu

F32 = jnp.float32
BF16 = jnp.bfloat16

EPS = 1e-6
N_META = 16
SEQ_ALIGN = 128
POOL_WINDOWS = (2, 4, 8, 16)
CONV_K = 3
SB_HEADS = 8
PEER_HEADS = 8
PEER_TOPK = 16

LANES = 128
NEG_INF = float("-inf")
SB_DEAD_LOG = -105.0

VMEM_LIMIT = 56 * 1024 * 1024


def _cparams(sem):
    return pltpu.CompilerParams(dimension_semantics=sem, vmem_limit_bytes=VMEM_LIMIT)


def _split_bf16(x):
    hi = x.astype(BF16)
    lo = (x - hi.astype(F32)).astype(BF16)
    return hi, lo


def _dot(a, b):
    return jnp.dot(a, b, preferred_element_type=F32)


def _rms_rows(x, g):
    ms = jnp.mean(x * x, axis=-1, keepdims=True)
    return x * lax.rsqrt(ms + EPS) * g


def _even_front_kernel(h_ref, g_ref, win_ref, qg_ref, kg_ref, cw_ref, bd_ref,
                       q_ref, kt_ref, v_ref, conv_ref, cbuf, *, tile, width, head_dim):
    i = pl.program_id(1)
    hn = _rms_rows(h_ref[...], g_ref[...])
    proj = _dot(hn.astype(BF16), win_ref[...])
    bd = bd_ref[...]

    def head_norm(t, gain):
        hi, lo = _split_bf16(t * t)
        ssq = _dot(hi, bd) + _dot(lo, bd)
        return t * lax.rsqrt(ssq * (1.0 / head_dim) + EPS) * gain

    q = head_norm(proj[:, 0:width], qg_ref[...]) * (head_dim ** -0.5)
    k = head_norm(proj[:, width:2 * width], kg_ref[...])
    q_ref[...] = q.astype(BF16)
    kt_ref[...] = k.T.astype(BF16)
    v_ref[...] = proj[:, 2 * width:3 * width].astype(BF16)

    gate_b = proj[:, 3 * width:4 * width]
    cu = proj[:, 4 * width:5 * width] * proj[:, 5 * width:6 * width]

    @pl.when(i == 0)
    def _():
        cbuf[0:8, :] = jnp.zeros((8, width), F32)

    cbuf[8:8 + tile, :] = cu
    y = cw_ref[2:3, :] * cu
    for s in range(1, CONV_K):
        y = y + cw_ref[CONV_K - 1 - s:CONV_K - s, :] * cbuf[8 - s:8 - s + tile, :]
    conv_ref[...] = (gate_b * y).astype(BF16)
    cbuf[0:8, :] = cu[tile - 8:tile, :]


def _even_front(h, norm_g, w_in, q_g, k_g, conv_w, tile):
    b, lp, d = h.shape
    width = w_in.shape[1] // 6
    head_dim = q_g.shape[0]
    heads = width // head_dim
    hid = jnp.arange(width) // head_dim
    bd = (hid[:, None] == hid[None, :]).astype(BF16)
    kern = functools.partial(_even_front_kernel, tile=tile, width=width, head_dim=head_dim)
    out_shapes = (
        jax.ShapeDtypeStruct((b, lp, width), BF16),
        jax.ShapeDtypeStruct((b, width, lp), BF16),
        jax.ShapeDtypeStruct((b, lp, width), BF16),
        jax.ShapeDtypeStruct((b, lp, width), BF16),
    )
    row_spec = pl.BlockSpec((None, tile, width), lambda bi, i: (bi, i, 0))
    full = lambda shape: pl.BlockSpec(shape, lambda bi, i: (0,) * len(shape))
    return pl.pallas_call(
        kern,
        grid=(b, lp // tile),
        in_specs=[
            pl.BlockSpec((None, tile, d), lambda bi, i: (bi, i, 0)),
            full((1, d)), full((d, 6 * width)), full((1, width)), full((1, width)),
            full((CONV_K, width)), full((width, width)),
        ],
        out_specs=(row_spec, pl.BlockSpec((None, width, tile), lambda bi, i: (bi, 0, i)), row_spec, row_spec),
        out_shape=out_shapes,
        scratch_shapes=[pltpu.VMEM((tile + 8, width), F32)],
        compiler_params=_cparams(("arbitrary", "arbitrary")),
        name="even_front",
    )(h, norm_g.reshape(1, d), w_in.astype(BF16), jnp.tile(q_g, heads).reshape(1, width),
      jnp.tile(k_g, heads).reshape(1, width), conv_w, bd)


def _stickbreak_kernel(q_ref, kt_ref, v_ref, uj_ref, o_ref, carry_ref, acc_ref, *, blk, head_dim, pairs):
    i = pl.program_id(1)
    pair = 2 * head_dim
    lane = lax.broadcasted_iota(jnp.int32, (blk, pair), 1)
    row = lax.broadcasted_iota(jnp.int32, (blk, blk), 0)
    col = lax.broadcasted_iota(jnp.int32, (blk, blk), 1)
    uj = uj_ref[...]

    carry_ref[...] = jnp.zeros(carry_ref.shape, F32)
    acc_ref[...] = jnp.zeros(acc_ref.shape, F32)

    def sweep_block(j):
        start = pl.multiple_of(j * blk, blk)
        causal = (j * blk + col) < (i * blk + row)
        heads = range(2 * pairs)
        zs = []
        for p in range(pairs):
            ps = slice(p * pair, (p + 1) * pair)
            q = q_ref[:, ps]
            kt = kt_ref[ps, pl.ds(start, blk)]
            zero = jnp.zeros_like(q)
            zs.append(_dot(jnp.where(lane < head_dim, q, zero), kt))
            zs.append(_dot(jnp.where(lane >= head_dim, q, zero), kt))
        cums, log_betas = [], []
        for hd in heads:
            z = zs[hd]
            l1p = jnp.log(1.0 + jnp.exp(-jnp.abs(z)))
            log_keep = jnp.where(causal, -(jnp.maximum(z, 0.0) + l1p), 0.0)
            log_betas.append(jnp.minimum(z, 0.0) - l1p)
            cums.append(_dot(log_keep.astype(BF16), uj))
        top = None
        for hd in heads:
            p = hd // 2
            vb = v_ref[pl.ds(start, blk), p * pair:(p + 1) * pair]
            carry = carry_ref[hd]
            later = cums[hd][:, :blk] + carry
            w = jnp.where(causal, jnp.exp(log_betas[hd] + later), 0.0)
            acc_ref[hd] += _dot(w.astype(BF16), vb)
            carry = carry + cums[hd][:, blk:]
            carry_ref[hd] = carry
            m = jnp.max(carry, axis=0, keepdims=True)
            top = m if top is None else jnp.maximum(top, m)
        return jnp.max(top)

    top0 = sweep_block(i)

    def cond(state):
        j, top = state
        return jnp.logical_and(j >= 0, top > SB_DEAD_LOG)

    def body(state):
        j, _ = state
        return j - 1, sweep_block(j)

    lax.while_loop(cond, body, (i - 1, top0))
    for p in range(pairs):
        o_ref[:, p * pair:(p + 1) * pair] = jnp.where(lane < head_dim, acc_ref[2 * p], acc_ref[2 * p + 1]).astype(BF16)


def _stickbreak_attention(q, kt, v, head_dim):
    b, lp, width = q.shape
    blk = SEQ_ALIGN
    pair = 2 * head_dim
    assert pair == LANES and width % pair == 0 and lp % blk == 0
    pairs = width // pair
    r = jnp.arange(blk)
    later_mat = (r[:, None] > r[None, :]).astype(BF16)
    uj = jnp.concatenate([later_mat, jnp.ones((blk, blk), BF16)], axis=1)
    kern = functools.partial(_stickbreak_kernel, blk=blk, head_dim=head_dim, pairs=pairs)
    resident = pl.Buffered(1)
    return pl.pallas_call(
        kern,
        grid=(b, lp // blk),
        in_specs=[
            pl.BlockSpec((None, blk, width), lambda bi, i: (bi, i, 0)),
            pl.BlockSpec((None, width, lp), lambda bi, i: (bi, 0, 0), pipeline_mode=resident),
            pl.BlockSpec((None, lp, width), lambda bi, i: (bi, 0, 0), pipeline_mode=resident),
            pl.BlockSpec((blk, 2 * blk), lambda bi, i: (0, 0)),
        ],
        out_specs=pl.BlockSpec((None, blk, width), lambda bi, i: (bi, i, 0)),
        out_shape=jax.ShapeDtypeStruct((b, lp, width), BF16),
        scratch_shapes=[pltpu.VMEM((2 * pairs, blk, blk), F32), pltpu.VMEM((2 * pairs, blk, pair), F32)],
        compiler_params=_cparams(("arbitrary", "arbitrary")),
        name="stickbreak_attention",
    )(q, kt, v, uj)


def _even_out_kernel(h_ref, att_ref, conv_ref, wa_ref, wc_ref, o_ref):
    o_ref[...] = h_ref[...] + _dot(att_ref[...], wa_ref[...]) + _dot(conv_ref[...], wc_ref[...])


def _even_out(h2, att2, conv2, w_out, tile):
    n, d = h2.shape
    width = att2.shape[1]
    w = w_out.astype(BF16)
    return pl.pallas_call(
        _even_out_kernel,
        grid=(n // tile,),
        in_specs=[
            pl.BlockSpec((tile, d), lambda i: (i, 0)),
            pl.BlockSpec((tile, width), lambda i: (i, 0)),
            pl.BlockSpec((tile, width), lambda i: (i, 0)),
            pl.BlockSpec((width, d), lambda i: (0, 0)),
            pl.BlockSpec((width, d), lambda i: (0, 0)),
        ],
        out_specs=pl.BlockSpec((tile, d), lambda i: (i, 0)),
        out_shape=jax.ShapeDtypeStruct((n, d), F32),
        compiler_params=_cparams(("arbitrary",)),
        name="even_out",
    )(h2, att2, conv2, w[:width], w[width:])


def _odd_kernel(h_ref, g_ref, pw_ref, ps_ref, o_ref, buf, *, tile, group):
    i = pl.program_id(1)
    halo = max(POOL_WINDOWS)
    x = h_ref[...]
    hn = _rms_rows(x, g_ref[...])

    @pl.when(i == 0)
    def _():
        buf[0:halo, :] = jnp.zeros((halo, hn.shape[1]), F32)

    buf[halo:halo + tile, :] = hn
    pos = i * tile + lax.broadcasted_iota(jnp.int32, (tile, group), 0)
    ys = []
    for gi, w in enumerate(POOL_WINDOWS):
        c0 = gi * group
        cur = hn[:, c0:c0 + group]
        s = cur
        for sft in range(1, w):
            s = s + buf[halo - sft:halo - sft + tile, c0:c0 + group]
        count = jnp.minimum(pos + 1, w).astype(F32)
        diff = s / count - cur
        ys.append(_dot(diff.astype(BF16), pw_ref[gi]))
    y = jnp.concatenate(ys, axis=1)
    o_ref[...] = x + y * ps_ref[...]
    buf[0:halo, :] = hn[tile - halo:tile, :]


def _odd_layer(h, norm_g, pool_w, pool_scale, tile):
    b, lp, d = h.shape
    groups, group, _ = pool_w.shape
    assert groups == len(POOL_WINDOWS) and groups * group == d
    kern = functools.partial(_odd_kernel, tile=tile, group=group)
    return pl.pallas_call(
        kern,
        grid=(b, lp // tile),
        in_specs=[
            pl.BlockSpec((None, tile, d), lambda bi, i: (bi, i, 0)),
            pl.BlockSpec((1, d), lambda bi, i: (0, 0)),
            pl.BlockSpec((groups, group, group), lambda bi, i: (0, 0, 0)),
            pl.BlockSpec((1, d), lambda bi, i: (0, 0)),
        ],
        out_specs=pl.BlockSpec((None, tile, d), lambda bi, i: (bi, i, 0)),
        out_shape=jax.ShapeDtypeStruct((b, lp, d), F32),
        scratch_shapes=[pltpu.VMEM((tile + max(POOL_WINDOWS), d), F32)],
        compiler_params=_cparams(("arbitrary", "arbitrary")),
        name="odd_layer",
    )(h, norm_g.reshape(1, d), pool_w.astype(BF16), pool_scale.reshape(1, d))


def _top16_rows(s, exact_ties):
    rows = s.shape[0]
    row_id = lax.broadcasted_iota(jnp.int32, s.shape, 0).astype(F32) if exact_ties else None
    rank = jnp.full(s.shape, float(PEER_TOPK), F32)
    vals = []
    for it in range(PEER_TOPK):
        m = jnp.max(s, axis=0, keepdims=True)
        hit = s == m
        if exact_ties:
            first = jnp.min(jnp.where(hit, row_id, float(rows)), axis=0, keepdims=True)
            hit = row_id == first
        rank = jnp.where(hit, float(it), rank)
        s = jnp.where(hit, NEG_INF, s)
        vals.append(m)
    return rank, vals


def _pair_partner_counts(v1, v2, exact_ties):
    k = PEER_TOPK
    half = k // 2
    v2_all = jnp.concatenate(v2, axis=0)
    v2_lo = v2_all[0:half]
    sub8 = lax.broadcasted_iota(jnp.int32, (half, LANES), 0)
    tiles = [v1[0] + v2_all]
    masked = [0]
    for a in range(1, half):
        bmax = k // (a + 1) - 1
        tiles.append(jnp.where(sub8 <= bmax, v1[a] + v2_lo, NEG_INF))
        masked.append(half - 1 - bmax)
    tiles.append(jnp.concatenate(v1[half:], axis=0) + v2[0])
    if exact_ties:
        sub16 = lax.broadcasted_iota(jnp.int32, (k, LANES), 0)
        flats = [sub16.astype(F32)] + [(sub8 + a * k).astype(F32) for a in range(1, half)]
        flats.append(((sub8 + half) * k).astype(F32))

    top = v1[0] + v2[0]
    denom = jnp.zeros((1, LANES), F32)
    big = float(k * k)
    for _ in range(k):
        m = tiles[0].max(axis=0, keepdims=True)
        for t in tiles[1:]:
            m = jnp.maximum(m, t.max(axis=0, keepdims=True))
        if exact_ties:
            first = None
            for t, f in zip(tiles, flats):
                c = jnp.min(jnp.where(t == m, f, big), axis=0, keepdims=True)
                first = c if first is None else jnp.minimum(first, c)
            tiles = [jnp.where(f == first, NEG_INF, t) for t, f in zip(tiles, flats)]
        else:
            tiles = [jnp.where(t == m, NEG_INF, t) for t in tiles]
        denom = denom + jnp.exp(m - top)
    gone = [jnp.where(t == NEG_INF, 1.0, 0.0) for t in tiles]
    counts = [jnp.sum(gone[a], axis=0, keepdims=True) - float(masked[a]) for a in range(half)]
    counts += [gone[half][r:r + 1] for r in range(half)]
    return counts, denom


def _route_chunk(s1, s2, exact_ties):
    k = float(PEER_TOPK)
    rank1, v1 = _top16_rows(s1, exact_ties)
    rank2, v2 = _top16_rows(s2, exact_ties)
    counts, denom = _pair_partner_counts(v1, v2, exact_ties)
    n1 = jnp.zeros(s1.shape, F32)
    for a in range(PEER_TOPK):
        n1 = jnp.where(rank1 == float(a), counts[a], n1)
    f1 = jnp.exp(s1 - v1[0])
    f2 = jnp.exp(s2 - v2[0]) / denom
    if exact_ties:
        bad = jnp.zeros((1, LANES), F32)
    else:
        picked1 = jnp.sum(jnp.where(rank1 < k, 1.0, 0.0), axis=0, keepdims=True)
        picked2 = jnp.sum(jnp.where(rank2 < k, 1.0, 0.0), axis=0, keepdims=True)
        pairs = counts[0]
        for c in counts[1:]:
            pairs = pairs + c
        ok = jnp.logical_and(jnp.logical_and(picked1 == k, picked2 == k), pairs == k)
        bad = jnp.where(ok, 0.0, 1.0)
    return n1, f1, rank2, f2, bad


def _peer_route_kernel(h_ref, g_ref, kwh_ref, kwl_ref,
                       hnt_ref, r2_ref, f2_ref, n1_ref, f1_ref, hi_scr, lo_scr, s_scr, *, tile, nkeys):
    hn = _rms_rows(h_ref[...], g_ref[...])
    hn_hi, hn_lo = _split_bf16(hn.T)
    hnt_ref[...] = hn_hi
    hi_scr[...] = hn_hi
    lo_scr[...] = hn_lo
    chunks = [slice(c * LANES, (c + 1) * LANES) for c in range(tile // LANES)]

    def head_scores(hh, slot):
        rows = pl.ds(pl.multiple_of(hh * 2 * nkeys, 2 * nkeys), 2 * nkeys)
        kwh = kwh_ref[rows, :]
        hi = hi_scr[...]
        s_scr[slot] = _dot(kwh, hi) + _dot(kwh, lo_scr[...]) + _dot(kwl_ref[rows, :], hi)

    head_scores(0, 0)

    def per_head(hh, _):
        slot = lax.rem(hh, 2)
        s1 = s_scr[slot, 0:nkeys, :]
        s2 = s_scr[slot, nkeys:2 * nkeys, :]
        head_scores(jnp.minimum(hh + 1, PEER_HEADS - 1), 1 - slot)

        def write(cs, n1, f1, rank2, f2):
            n1_ref[hh, :, cs] = n1
            f1_ref[hh, :, cs] = f1
            r2_ref[hh, :, cs] = rank2.astype(BF16)
            f2_ref[hh, :, cs] = f2.astype(BF16)

        bad = None
        for cs in chunks:
            n1, f1, rank2, f2, bad_c = _route_chunk(s1[:, cs], s2[:, cs], exact_ties=False)
            write(cs, n1, f1, rank2, f2)
            bad = bad_c if bad is None else jnp.maximum(bad, bad_c)

        @pl.when(jnp.max(bad) > 0.0)
        def _():
            for cs in chunks:
                write(cs, *_route_chunk(s1[:, cs], s2[:, cs], exact_ties=True)[:4])
        return 0

    lax.fori_loop(0, PEER_HEADS, per_head, 0)


def _split3_bf16(x):
    hi = x.astype(BF16)
    r1 = x - hi.astype(F32)
    mid = r1.astype(BF16)
    lo = (r1 - mid.astype(F32)).astype(BF16)
    return hi, mid, lo


def _fold_keys_kernel(k_ref, wq_ref, o_ref):
    kh, km, kl = _split3_bf16(k_ref[...])
    wh, wm, wl = _split3_bf16(wq_ref[...])
    o_ref[...] = (_dot(kh, wh) + (_dot(kh, wm) + _dot(km, wh))
                  + (_dot(km, wm) + _dot(kh, wl) + _dot(kl, wh)))


def _fold_keys(w_query, sub_keys):
    d = w_query.shape[0]
    _, heads, nkeys, half = sub_keys.shape
    keys = sub_keys.transpose(1, 0, 2, 3).reshape(heads * 2, nkeys, half)
    return pl.pallas_call(
        _fold_keys_kernel,
        grid=(heads * 2,),
        in_specs=[pl.BlockSpec((None, nkeys, half), lambda r: (r, 0, 0)),
                  pl.BlockSpec((half, d), lambda r: (r, 0))],
        out_specs=pl.BlockSpec((nkeys, d), lambda r: (r, 0)),
        out_shape=jax.ShapeDtypeStruct((heads * 2 * nkeys, d), F32),
        compiler_params=_cparams(("arbitrary",)),
        name="peer_fold_keys",
    )(keys, w_query.T)


def _peer_route(h2, norm_g, w_query, sub_keys, tile):
    n, d = h2.shape
    _, heads, nkeys, half = sub_keys.shape
    assert heads == PEER_HEADS and nkeys == half == LANES
    srows = heads * 2 * nkeys
    kw_hi, kw_lo = _split_bf16(_fold_keys(w_query, sub_keys))
    kern = functools.partial(_peer_route_kernel, tile=tile, nkeys=nkeys)
    tab = lambda dt: jax.ShapeDtypeStruct((heads, nkeys, n), dt)
    tab_spec = pl.BlockSpec((heads, nkeys, tile), lambda i: (0, 0, i))
    const = lambda shape: pl.BlockSpec(shape, lambda i: (0,) * len(shape))
    return pl.pallas_call(
        kern,
        grid=(n // tile,),
        in_specs=[
            pl.BlockSpec((tile, d), lambda i: (i, 0)),
            const((1, d)), const((srows, d)), const((srows, d)),
        ],
        out_specs=(pl.BlockSpec((d, tile), lambda i: (0, i)), tab_spec, tab_spec, tab_spec, tab_spec),
        out_shape=(jax.ShapeDtypeStruct((d, n), BF16), tab(BF16), tab(BF16), tab(F32), tab(F32)),
        scratch_shapes=[pltpu.VMEM((d, tile), BF16), pltpu.VMEM((d, tile), BF16),
                        pltpu.VMEM((2, 2 * nkeys, tile), F32)],
        compiler_params=_cparams(("arbitrary",)),
        name="peer_route",
    )(h2, norm_g.reshape(1, d), kw_hi, kw_lo)


def _gelu_times_gate(a, gate):
    k = -2.0 * math.sqrt(2.0 / math.pi) * math.log2(math.e)
    u = a * ((a * a) * (k * 0.044715) + k)
    return a.astype(BF16) * gate / (1.0 + jnp.exp2(u.astype(BF16)))


def _pack_row_pairs(x):
    m2, n = x.shape
    return lax.bitcast_convert_type(x.reshape(m2 // 2, 2, n).swapaxes(1, 2), jnp.uint32)


def _peer_expert_kernel(hnt_ref, u_ref, vt_ref, r2_ref, f2_ref, n1_ref, f1_ref, h_ref,
                        o_ref, acc_ref, ga_ref, *, tile, nkeys, rows_per_step):
    e = pl.program_id(1)

    @pl.when(e == 0)
    def _():
        acc_ref[...] = jnp.zeros(acc_ref.shape, F32)

    pre = _dot(pltpu.bitcast(u_ref[...], BF16), hnt_ref[...])
    sub = 16
    for r in range(rows_per_step):
        bcast = []
        for hh in range(PEER_HEADS):
            n1 = jnp.broadcast_to(n1_ref[hh, r:r + 1, :], (sub, tile)).astype(BF16)
            f1 = jnp.broadcast_to(f1_ref[hh, r:r + 1, :], (sub, tile)).astype(BF16)
            bcast.append((n1, f1))
        for g in range(nkeys // sub):
            rs = slice(g * sub, (g + 1) * sub)
            gate = jnp.zeros((sub, tile), BF16)
            for hh in range(PEER_HEADS):
                n1, f1 = bcast[hh]
                gate = gate + jnp.where(r2_ref[hh, rs, :] < n1, f2_ref[hh, rs, :], jnp.zeros((), BF16)) * f1
            o0 = r * nkeys + g * sub
            ga_ref[o0:o0 + sub, :] = _gelu_times_gate(pre[o0:o0 + sub, :], gate)
    acc_ref[...] += _dot(pltpu.bitcast(vt_ref[...], BF16), ga_ref[...])

    @pl.when(e == pl.num_programs(1) - 1)
    def _():
        o_ref[...] = h_ref[...] + acc_ref[...].T


def _peer_experts(h2, hnt, r2, f2, n1, f1, expert_u, expert_v, tile, rows_per_step):
    n, d = h2.shape
    heads, nkeys, _ = r2.shape
    n_exp = expert_u.shape[0]
    te = rows_per_step * nkeys
    assert n_exp == nkeys * nkeys and n_exp % te == 0 and rows_per_step % 8 == 0
    u = _pack_row_pairs(expert_u.astype(BF16))
    vt = _pack_row_pairs(expert_v.T.astype(BF16))
    kern = functools.partial(_peer_expert_kernel, tile=tile, nkeys=nkeys, rows_per_step=rows_per_step)
    tab2 = pl.BlockSpec((heads, nkeys, tile), lambda t, e: (0, 0, t))
    tab1 = pl.BlockSpec((heads, rows_per_step, tile), lambda t, e: (0, e, t))
    return pl.pallas_call(
        kern,
        grid=(n // tile, n_exp // te),
        in_specs=[
            pl.BlockSpec((d, tile), lambda t, e: (0, t)),
            pl.BlockSpec((te // 2, d), lambda t, e: (e, 0)),
            pl.BlockSpec((d // 2, te), lambda t, e: (0, e)),
            tab2, tab2, tab1, tab1,
            pl.BlockSpec((tile, d), lambda t, e: (t, 0)),
        ],
        out_specs=pl.BlockSpec((tile, d), lambda t, e: (t, 0)),
        out_shape=jax.ShapeDtypeStruct((n, d), F32),
        scratch_shapes=[pltpu.VMEM((d, tile), F32), pltpu.VMEM((te, tile), BF16)],
        compiler_params=_cparams(("arbitrary", "arbitrary")),
        name="peer_experts",
    )(hnt, u, vt, r2, f2, n1, f1, h2)


def _peer_ffn(h2, norm_g, w_query, sub_keys, expert_u, expert_v, route_tile, expert_tile):
    hnt, r2, f2, n1, f1 = _peer_route(h2, norm_g, w_query, sub_keys, route_tile)
    return _peer_experts(h2, hnt, r2, f2, n1, f1, expert_u, expert_v, expert_tile, rows_per_step=16)


def _pick_tile(n, candidates):
    for c in candidates:
        if n % c == 0:
            return c
    raise ValueError(f"no tile in {candidates} divides {n}")


def kernel(x, meta_tokens, even_norm_g, even_w_in, even_q_norm_g, even_k_norm_g, even_conv_w, even_w_out,
           odd_norm_g, odd_pool_w, odd_pool_scale, ffn_norm_g, peer_w_query, peer_sub_keys, peer_u, peer_v):
    b, seq, d = x.shape
    total = N_META + seq
    lp = -(-total // SEQ_ALIGN) * SEQ_ALIGN
    meta = jnp.broadcast_to(meta_tokens[None].astype(x.dtype), (b, N_META, d))
    h = jnp.concatenate([meta, x], axis=1)
    h = jnp.pad(h, ((0, 0), (0, lp - total), (0, 0)))
    n = b * lp
    seq_tile = _pick_tile(lp, (384, 256, 128))
    flat_tile = _pick_tile(n, (768, 512, 384, 256, 128))
    route_tile = _pick_tile(n, (256, 128))
    depth = ffn_norm_g.shape[0]
    head_dim = even_q_norm_g.shape[1]
    for layer in range(depth):
        i = layer // 2
        if layer % 2 == 0:
            q, kt, v, conv = _even_front(h, even_norm_g[i], even_w_in[i], even_q_norm_g[i], even_k_norm_g[i],
                                         even_conv_w[i], seq_tile)
            att = _stickbreak_attention(q, kt, v, head_dim)
            width = att.shape[-1]
            h2 = _even_out(h.reshape(n, d), att.reshape(n, width), conv.reshape(n, width), even_w_out[i], flat_tile)
        else:
            h2 = _odd_layer(h, odd_norm_g[i], odd_pool_w[i], odd_pool_scale[i], seq_tile).reshape(n, d)
        h2 = _peer_ffn(h2, ffn_norm_g[layer], peer_w_query[layer], peer_sub_keys[layer], peer_u[layer], peer_v[layer],
                       route_tile, flat_tile)
        h = h2.reshape(b, lp, d)
    return h[:, N_META:N_META + seq]
```

```python
import functools
import math

import jax
import jax.numpy as jnp
from jax import lax
from jax.experimental import pallas as pl
from jax.experimental.pallas import tpu as pltpu

F32 = jnp.float32
BF16 = jnp.bfloat16

EPS = 1e-6
N_META = 16
SEQ_ALIGN = 128
POOL_WINDOWS = (2, 4, 8, 16)
CONV_K = 3
SB_HEADS = 8
PEER_HEADS = 8
PEER_TOPK = 16

LANES = 128
SUBLANES = 8
NEG_INF = float("-inf")
SB_DEAD_LOG = -105.0

VMEM_LIMIT = 56 * 1024 * 1024


def _cparams(sem):
    return pltpu.CompilerParams(dimension_semantics=sem, vmem_limit_bytes=VMEM_LIMIT)


def _split_bf16(x):
    hi = x.astype(BF16)
    lo = (x - hi.astype(F32)).astype(BF16)
    return hi, lo


def _dot(a, b):
    return jnp.dot(a, b, preferred_element_type=F32)


def _rms_rows(x, g):
    ms = jnp.mean(x * x, axis=-1, keepdims=True)
    return x * lax.rsqrt(ms + EPS) * g


def _even_front_kernel(h_ref, g_ref, win_ref, qg_ref, kg_ref, cw_ref, bd_ref,
                       q_ref, kt_ref, v_ref, conv_ref, cbuf, *, tile, width, head_dim):
    i = pl.program_id(1)
    hn = _rms_rows(h_ref[...], g_ref[...])
    proj = _dot(hn.astype(BF16), win_ref[...])
    bd = bd_ref[...]

    def head_norm(t, gain):
        hi, lo = _split_bf16(t * t)
        ssq = _dot(hi, bd) + _dot(lo, bd)
        return t * lax.rsqrt(ssq * (1.0 / head_dim) + EPS) * gain

    q = head_norm(proj[:, 0:width], qg_ref[...]) * (head_dim ** -0.5)
    k = head_norm(proj[:, width:2 * width], kg_ref[...])
    q_ref[...] = q.astype(BF16)
    kt_ref[...] = k.T.astype(BF16)
    v_ref[...] = proj[:, 2 * width:3 * width].astype(BF16)

    gate_b = proj[:, 3 * width:4 * width]
    cu = proj[:, 4 * width:5 * width] * proj[:, 5 * width:6 * width]

    @pl.when(i == 0)
    def _():
        cbuf[0:8, :] = jnp.zeros((8, width), F32)

    cbuf[8:8 + tile, :] = cu
    y = cw_ref[2:3, :] * cu
    for s in range(1, CONV_K):
        y = y + cw_ref[CONV_K - 1 - s:CONV_K - s, :] * cbuf[8 - s:8 - s + tile, :]
    conv_ref[...] = (gate_b * y).astype(BF16)
    cbuf[0:8, :] = cu[tile - 8:tile, :]


def _even_front(h, norm_g, w_in, q_g, k_g, conv_w, tile):
    b, lp, d = h.shape
    width = w_in.shape[1] // 6
    head_dim = q_g.shape[0]
    heads = width // head_dim
    hid = jnp.arange(width) // head_dim
    bd = (hid[:, None] == hid[None, :]).astype(BF16)
    kern = functools.partial(_even_front_kernel, tile=tile, width=width, head_dim=head_dim)
    out_shapes = (
        jax.ShapeDtypeStruct((b, lp, width), BF16),
        jax.ShapeDtypeStruct((b, width, lp), BF16),
        jax.ShapeDtypeStruct((b, lp, width), BF16),
        jax.ShapeDtypeStruct((b, lp, width), BF16),
    )
    row_spec = pl.BlockSpec((None, tile, width), lambda bi, i: (bi, i, 0))
    full = lambda shape: pl.BlockSpec(shape, lambda bi, i: (0,) * len(shape))
    return pl.pallas_call(
        kern,
        grid=(b, lp // tile),
        in_specs=[
            pl.BlockSpec((None, tile, d), lambda bi, i: (bi, i, 0)),
            full((1, d)), full((d, 6 * width)), full((1, width)), full((1, width)),
            full((CONV_K, width)), full((width, width)),
        ],
        out_specs=(row_spec, pl.BlockSpec((None, width, tile), lambda bi, i: (bi, 0, i)), row_spec, row_spec),
        out_shape=out_shapes,
        scratch_shapes=[pltpu.VMEM((tile + 8, width), F32)],
        compiler_params=_cparams(("arbitrary", "arbitrary")),
        name="even_front",
    )(h, norm_g.reshape(1, d), w_in.astype(BF16), jnp.tile(q_g, heads).reshape(1, width),
      jnp.tile(k_g, heads).reshape(1, width), conv_w, bd)


def _stickbreak_kernel(q_ref, kt_ref, v_ref, uj_ref, o_ref, carry_ref, acc_ref, *, blk, head_dim, pairs):
    i = pl.program_id(1)
    pair = 2 * head_dim
    lane = lax.broadcasted_iota(jnp.int32, (blk, pair), 1)
    row = lax.broadcasted_iota(jnp.int32, (blk, blk), 0)
    col = lax.broadcasted_iota(jnp.int32, (blk, blk), 1)
    uj = uj_ref[...]

    carry_ref[...] = jnp.zeros(carry_ref.shape, F32)
    acc_ref[...] = jnp.zeros(acc_ref.shape, F32)

    def sweep_block(j):
        start = pl.multiple_of(j * blk, blk)
        causal = (j * blk + col) < (i * blk + row)
        heads = range(2 * pairs)
        zs = []
        for p in range(pairs):
            ps = slice(p * pair, (p + 1) * pair)
            q = q_ref[:, ps]
            kt = kt_ref[ps, pl.ds(start, blk)]
            zero = jnp.zeros_like(q)
            zs.append(_dot(jnp.where(lane < head_dim, q, zero), kt))
            zs.append(_dot(jnp.where(lane >= head_dim, q, zero), kt))
        cums, log_betas = [], []
        for hd in heads:
            z = zs[hd]
            l1p = jnp.log(1.0 + jnp.exp(-jnp.abs(z)))
            log_keep = jnp.where(causal, -(jnp.maximum(z, 0.0) + l1p), 0.0)
            log_betas.append(jnp.minimum(z, 0.0) - l1p)
            cums.append(_dot(log_keep.astype(BF16), uj))
        top = None
        for hd in heads:
            p = hd // 2
            vb = v_ref[pl.ds(start, blk), p * pair:(p + 1) * pair]
            carry = carry_ref[hd]
            later = cums[hd][:, :blk] + carry
            w = jnp.where(causal, jnp.exp(log_betas[hd] + later), 0.0)
            acc_ref[hd] += _dot(w.astype(BF16), vb)
            carry = carry + cums[hd][:, blk:]
            carry_ref[hd] = carry
            m = jnp.max(carry, axis=0, keepdims=True)
            top = m if top is None else jnp.maximum(top, m)
        return jnp.max(top)

    top0 = sweep_block(i)

    def cond(state):
        j, top = state
        return jnp.logical_and(j >= 0, top > SB_DEAD_LOG)

    def body(state):
        j, _ = state
        return j - 1, sweep_block(j)

    lax.while_loop(cond, body, (i - 1, top0))
    for p in range(pairs):
        o_ref[:, p * pair:(p + 1) * pair] = jnp.where(lane < head_dim, acc_ref[2 * p], acc_ref[2 * p + 1]).astype(BF16)


def _stickbreak_attention(q, kt, v, head_dim):
    b, lp, width = q.shape
    blk = SEQ_ALIGN
    pair = 2 * head_dim
    assert pair == LANES and width % pair == 0 and lp % blk == 0
    pairs = width // pair
    r = jnp.arange(blk)
    later_mat = (r[:, None] > r[None, :]).astype(BF16)
    uj = jnp.concatenate([later_mat, jnp.ones((blk, blk), BF16)], axis=1)
    kern = functools.partial(_stickbreak_kernel, blk=blk, head_dim=head_dim, pairs=pairs)
    resident = pl.Buffered(1)
    return pl.pallas_call(
        kern,
        grid=(b, lp // blk),
        in_specs=[
            pl.BlockSpec((None, blk, width), lambda bi, i: (bi, i, 0)),
            pl.BlockSpec((None, width, lp), lambda bi, i: (bi, 0, 0), pipeline_mode=resident),
            pl.BlockSpec((None, lp, width), lambda bi, i: (bi, 0, 0), pipeline_mode=resident),
            pl.BlockSpec((blk, 2 * blk), lambda bi, i: (0, 0)),
        ],
        out_specs=pl.BlockSpec((None, blk, width), lambda bi, i: (bi, i, 0)),
        out_shape=jax.ShapeDtypeStruct((b, lp, width), BF16),
        scratch_shapes=[pltpu.VMEM((2 * pairs, blk, blk), F32), pltpu.VMEM((2 * pairs, blk, pair), F32)],
        compiler_params=_cparams(("arbitrary", "arbitrary")),
        name="stickbreak_attention",
    )(q, kt, v, uj)


def _even_out_kernel(h_ref, att_ref, conv_ref, wa_ref, wc_ref, o_ref):
    o_ref[...] = h_ref[...] + _dot(att_ref[...], wa_ref[...]) + _dot(conv_ref[...], wc_ref[...])


def _even_out(h2, att2, conv2, w_out, tile):
    n, d = h2.shape
    width = att2.shape[1]
    w = w_out.astype(BF16)
    return pl.pallas_call(
        _even_out_kernel,
        grid=(n // tile,),
        in_specs=[
            pl.BlockSpec((tile, d), lambda i: (i, 0)),
            pl.BlockSpec((tile, width), lambda i: (i, 0)),
            pl.BlockSpec((tile, width), lambda i: (i, 0)),
            pl.BlockSpec((width, d), lambda i: (0, 0)),
            pl.BlockSpec((width, d), lambda i: (0, 0)),
        ],
        out_specs=pl.BlockSpec((tile, d), lambda i: (i, 0)),
        out_shape=jax.ShapeDtypeStruct((n, d), F32),
        compiler_params=_cparams(("arbitrary",)),
        name="even_out",
    )(h2, att2, conv2, w[:width], w[width:])


def _odd_kernel(h_ref, g_ref, pw_ref, ps_ref, o_ref, buf, *, tile, group):
    i = pl.program_id(1)
    halo = max(POOL_WINDOWS)
    x = h_ref[...]
    hn = _rms_rows(x, g_ref[...])

    @pl.when(i == 0)
    def _():
        buf[0:halo, :] = jnp.zeros((halo, hn.shape[1]), F32)

    buf[halo:halo + tile, :] = hn
    pos = i * tile + lax.broadcasted_iota(jnp.int32, (tile, group), 0)
    ys = []
    for gi, w in enumerate(POOL_WINDOWS):
        c0 = gi * group
        cur = hn[:, c0:c0 + group]
        s = cur
        for sft in range(1, w):
            s = s + buf[halo - sft:halo - sft + tile, c0:c0 + group]
        count = jnp.minimum(pos + 1, w).astype(F32)
        diff = s / count - cur
        ys.append(_dot(diff.astype(BF16), pw_ref[gi]))
    y = jnp.concatenate(ys, axis=1)
    o_ref[...] = x + y * ps_ref[...]
    buf[0:halo, :] = hn[tile - halo:tile, :]


def _odd_layer(h, norm_g, pool_w, pool_scale, tile):
    b, lp, d = h.shape
    groups, group, _ = pool_w.shape
    assert groups == len(POOL_WINDOWS) and groups * group == d
    kern = functools.partial(_odd_kernel, tile=tile, group=group)
    return pl.pallas_call(
        kern,
        grid=(b, lp // tile),
        in_specs=[
            pl.BlockSpec((None, tile, d), lambda bi, i: (bi, i, 0)),
            pl.BlockSpec((1, d), lambda bi, i: (0, 0)),
            pl.BlockSpec((groups, group, group), lambda bi, i: (0, 0, 0)),
            pl.BlockSpec((1, d), lambda bi, i: (0, 0)),
        ],
        out_specs=pl.BlockSpec((None, tile, d), lambda bi, i: (bi, i, 0)),
        out_shape=jax.ShapeDtypeStruct((b, lp, d), F32),
        scratch_shapes=[pltpu.VMEM((tile + max(POOL_WINDOWS), d), F32)],
        compiler_params=_cparams(("arbitrary", "arbitrary")),
        name="odd_layer",
    )(h, norm_g.reshape(1, d), pool_w.astype(BF16), pool_scale.reshape(1, d))


def _fold_leading(x, op):
    parts = [x[i] for i in range(x.shape[0])]
    while len(parts) > 1:
        parts = [op(parts[i], parts[i + 1]) for i in range(0, len(parts) - 1, 2)] + parts[len(parts) & ~1:]
    return parts[0]


def _fold_rows(x, op):
    for shift in (4, 2, 1):
        x = op(x, pltpu.roll(x, shift, 0))
    return x


def _top16_rows(s, exact_ties, want_rank):
    groups = s.shape[0]
    if exact_ties:
        key_id = (lax.broadcasted_iota(jnp.int32, s.shape, 0) * SUBLANES
                  + lax.broadcasted_iota(jnp.int32, s.shape, 1)).astype(F32)
    rank = jnp.full(s.shape, float(PEER_TOPK), F32) if want_rank else None
    vals = []
    for it in range(PEER_TOPK):
        m = _fold_rows(_fold_leading(s, jnp.maximum), jnp.maximum)
        hit = s == m[None]
        if exact_ties:
            first = _fold_leading(jnp.where(hit, key_id, float(groups * SUBLANES)), jnp.minimum)
            hit = key_id == _fold_rows(first, jnp.minimum)[None]
        if want_rank:
            rank = jnp.where(hit, float(it), rank)
        s = jnp.where(hit, NEG_INF, s)
        vals.append(m)
    return rank, vals, s


def _pair_partner_counts(v1, v2, exact_ties):
    k = PEER_TOPK
    half = k // 2
    assert half == SUBLANES
    sub = lax.broadcasted_iota(jnp.int32, (SUBLANES, LANES), 0)

    def as_rows(vals):
        out = vals[0]
        for r in range(1, SUBLANES):
            out = jnp.where(sub == r, vals[r], out)
        return out

    v2_lo, v2_hi = as_rows(v2[:half]), as_rows(v2[half:])
    tiles = [v1[0] + v2_lo, v1[0] + v2_hi]
    flat_base = [0, half]
    masked = [0, 0]
    for a in range(1, half):
        bmax = k // (a + 1) - 1
        tiles.append(jnp.where(sub <= bmax, v1[a] + v2_lo, NEG_INF))
        flat_base.append(a * k)
        masked.append(half - 1 - bmax)
    tiles.append(as_rows(v1[half:]) + v2[0])
    if exact_ties:
        flats = [(sub + base).astype(F32) for base in flat_base] + [((sub + half) * k).astype(F32)]

    top = v1[0] + v2[0]
    denom = jnp.zeros((SUBLANES, LANES), F32)
    big = float(k * k)
    for _ in range(k):
        m = tiles[0]
        for t in tiles[1:]:
            m = jnp.maximum(m, t)
        m = _fold_rows(m, jnp.maximum)
        if exact_ties:
            first = None
            for t, f in zip(tiles, flats):
                c = jnp.where(t == m, f, big)
                first = c if first is None else jnp.minimum(first, c)
            first = _fold_rows(first, jnp.minimum)
            tiles = [jnp.where(f == first, NEG_INF, t) for t, f in zip(tiles, flats)]
        else:
            tiles = [jnp.where(t == m, NEG_INF, t) for t in tiles]
        denom = denom + jnp.exp(m - top)
    gone = [jnp.where(t == NEG_INF, 1.0, 0.0) for t in tiles]
    counts = [_fold_rows(gone[0] + gone[1], jnp.add)]
    counts += [_fold_rows(gone[a + 1], jnp.add) - float(masked[a + 1]) for a in range(1, half)]
    counts += [_fold_rows(jnp.where(sub == r, gone[-1], 0.0), jnp.add) for r in range(half)]
    return counts, denom


def _route_chunk(s1, s2, exact_ties):
    k = float(PEER_TOPK)
    rank1, v1, left1 = _top16_rows(s1, exact_ties, want_rank=exact_ties)
    rank2, v2, left2 = _top16_rows(s2, exact_ties, want_rank=True)
    counts, denom = _pair_partner_counts(v1, v2, exact_ties)
    n1 = jnp.zeros(s1.shape, F32)
    for a in range(PEER_TOPK):
        is_a = (rank1 == float(a)) if exact_ties else (s1 == v1[a][None])
        n1 = jnp.where(is_a, counts[a][None], n1)
    f1 = jnp.exp(s1 - v1[0][None])
    f2 = jnp.exp(s2 - v2[0][None]) / denom[None]
    if exact_ties:
        bad = jnp.zeros((SUBLANES, LANES), F32)
    else:
        def picked(left):
            return _fold_rows(_fold_leading(jnp.where(left == NEG_INF, 1.0, 0.0), jnp.add), jnp.add)

        pairs = counts[0]
        for c in counts[1:]:
            pairs = pairs + c
        ok = jnp.logical_and(jnp.logical_and(picked(left1) == k, picked(left2) == k), pairs == k)
        bad = jnp.where(ok, 0.0, 1.0)
    return n1, f1, rank2, f2, bad


def _peer_route_kernel(h_ref, g_ref, kwh_ref, kwl_ref,
                       hnt_ref, r2_ref, f2_ref, n1_ref, f1_ref, hi_scr, lo_scr, s_scr, *, tile, nkeys):
    hn = _rms_rows(h_ref[...], g_ref[...])
    hn_hi, hn_lo = _split_bf16(hn.T)
    hnt_ref[...] = hn_hi
    hi_scr[...] = hn_hi
    lo_scr[...] = hn_lo
    chunks = [slice(c * LANES, (c + 1) * LANES) for c in range(tile // LANES)]

    def head_scores(hh, slot):
        rows = pl.ds(pl.multiple_of(hh * 2 * nkeys, 2 * nkeys), 2 * nkeys)
        kwh = kwh_ref[rows, :]
        hi = hi_scr[...]
        s_scr[slot] = _dot(kwh, hi) + _dot(kwh, lo_scr[...]) + _dot(kwl_ref[rows, :], hi)

    head_scores(0, 0)

    def per_head(hh, _):
        slot = lax.rem(hh, 2)
        s1 = s_scr[slot, 0:nkeys, :]
        s2 = s_scr[slot, nkeys:2 * nkeys, :]
        head_scores(jnp.minimum(hh + 1, PEER_HEADS - 1), 1 - slot)

        def keys_by_tile(x):
            return x.reshape(nkeys // SUBLANES, SUBLANES, LANES)

        def write(cs, n1, f1, rank2, f2):
            n1_ref[hh, :, cs] = n1.reshape(nkeys, LANES)
            f1_ref[hh, :, cs] = f1.reshape(nkeys, LANES)
            r2_ref[hh, :, cs] = rank2.reshape(nkeys, LANES).astype(BF16)
            f2_ref[hh, :, cs] = f2.reshape(nkeys, LANES).astype(BF16)

        def route(cs, exact_ties):
            return _route_chunk(keys_by_tile(s1[:, cs]), keys_by_tile(s2[:, cs]), exact_ties)

        bad = None
        for cs in chunks:
            n1, f1, rank2, f2, bad_c = route(cs, exact_ties=False)
            write(cs, n1, f1, rank2, f2)
            bad = bad_c if bad is None else jnp.maximum(bad, bad_c)

        @pl.when(jnp.max(bad) > 0.0)
        def _():
            for cs in chunks:
                write(cs, *route(cs, exact_ties=True)[:4])
        return 0

    lax.fori_loop(0, PEER_HEADS, per_head, 0)


def _split3_bf16(x):
    hi = x.astype(BF16)
    r1 = x - hi.astype(F32)
    mid = r1.astype(BF16)
    lo = (r1 - mid.astype(F32)).astype(BF16)
    return hi, mid, lo


def _fold_keys_kernel(k_ref, wq_ref, o_ref):
    kh, km, kl = _split3_bf16(k_ref[...])
    wh, wm, wl = _split3_bf16(wq_ref[...])
    o_ref[...] = (_dot(kh, wh) + (_dot(kh, wm) + _dot(km, wh))
                  + (_dot(km, wm) + _dot(kh, wl) + _dot(kl, wh)))


def _fold_keys(w_query, sub_keys):
    d = w_query.shape[0]
    _, heads, nkeys, half = sub_keys.shape
    keys = sub_keys.transpose(1, 0, 2, 3).reshape(heads * 2, nkeys, half)
    return pl.pallas_call(
        _fold_keys_kernel,
        grid=(heads * 2,),
        in_specs=[pl.BlockSpec((None, nkeys, half), lambda r: (r, 0, 0)),
                  pl.BlockSpec((half, d), lambda r: (r, 0))],
        out_specs=pl.BlockSpec((nkeys, d), lambda r: (r, 0)),
        out_shape=jax.ShapeDtypeStruct((heads * 2 * nkeys, d), F32),
        compiler_params=_cparams(("arbitrary",)),
        name="peer_fold_keys",
    )(keys, w_query.T)


def _peer_route(h2, norm_g, w_query, sub_keys, tile):
    n, d = h2.shape
    _, heads, nkeys, half = sub_keys.shape
    assert heads == PEER_HEADS and nkeys == half == LANES
    srows = heads * 2 * nkeys
    kw_hi, kw_lo = _split_bf16(_fold_keys(w_query, sub_keys))
    kern = functools.partial(_peer_route_kernel, tile=tile, nkeys=nkeys)
    tab = lambda dt: jax.ShapeDtypeStruct((heads, nkeys, n), dt)
    tab_spec = pl.BlockSpec((heads, nkeys, tile), lambda i: (0, 0, i))
    const = lambda shape: pl.BlockSpec(shape, lambda i: (0,) * len(shape))
    return pl.pallas_call(
        kern,
        grid=(n // tile,),
        in_specs=[
            pl.BlockSpec((tile, d), lambda i: (i, 0)),
            const((1, d)), const((srows, d)), const((srows, d)),
        ],
        out_specs=(pl.BlockSpec((d, tile), lambda i: (0, i)), tab_spec, tab_spec, tab_spec, tab_spec),
        out_shape=(jax.ShapeDtypeStruct((d, n), BF16), tab(BF16), tab(BF16), tab(F32), tab(F32)),
        scratch_shapes=[pltpu.VMEM((d, tile), BF16), pltpu.VMEM((d, tile), BF16),
                        pltpu.VMEM((2, 2 * nkeys, tile), F32)],
        compiler_params=_cparams(("arbitrary",)),
        name="peer_route",
    )(h2, norm_g.reshape(1, d), kw_hi, kw_lo)


def _gelu_times_gate(a, gate):
    k = -2.0 * math.sqrt(2.0 / math.pi) * math.log2(math.e)
    u = a * ((a * a) * (k * 0.044715) + k)
    return a.astype(BF16) * gate / (1.0 + jnp.exp2(u.astype(BF16)))


def _pack_rows_kernel(x_ref, o_ref):
    o_ref[...] = pltpu.bitcast(x_ref[...].astype(BF16), jnp.uint32)


def _pack_cols_kernel(x_ref, o_ref):
    o_ref[...] = pltpu.bitcast(x_ref[...].T.astype(BF16), jnp.uint32)


def _pack_expert_weights(expert_u, expert_v, rows):
    n_exp, d = expert_u.shape
    grid = (n_exp // rows,)
    in_spec = pl.BlockSpec((rows, d), lambda i: (i, 0))
    u = pl.pallas_call(
        _pack_rows_kernel, grid=grid, in_specs=[in_spec],
        out_specs=pl.BlockSpec((rows // 2, d), lambda i: (i, 0)),
        out_shape=jax.ShapeDtypeStruct((n_exp // 2, d), jnp.uint32),
        compiler_params=_cparams(("arbitrary",)), name="peer_pack_u",
    )(expert_u)
    vt = pl.pallas_call(
        _pack_cols_kernel, grid=grid, in_specs=[in_spec],
        out_specs=pl.BlockSpec((d // 2, rows), lambda i: (0, i)),
        out_shape=jax.ShapeDtypeStruct((d // 2, n_exp), jnp.uint32),
        compiler_params=_cparams(("arbitrary",)), name="peer_pack_vt",
    )(expert_v)
    return u, vt


def _peer_expert_kernel(hnt_ref, u_ref, vt_ref, r2_ref, f2_ref, n1_ref, f1_ref, h_ref,
                        o_ref, acc_ref, ga_ref, *, tile, nkeys, rows_per_step):
    e = pl.program_id(1)

    @pl.when(e == 0)
    def _():
        acc_ref[...] = jnp.zeros(acc_ref.shape, F32)

    pre = _dot(pltpu.bitcast(u_ref[...], BF16), hnt_ref[...])
    sub = 16
    for r in range(rows_per_step):
        bcast = []
        for hh in range(PEER_HEADS):
            n1 = jnp.broadcast_to(n1_ref[hh, r:r + 1, :], (sub, tile)).astype(BF16)
            f1 = jnp.broadcast_to(f1_ref[hh, r:r + 1, :], (sub, tile)).astype(BF16)
            bcast.append((n1, f1))
        for g in range(nkeys // sub):
            rs = slice(g * sub, (g + 1) * sub)
            gate = jnp.zeros((sub, tile), BF16)
            for hh in range(PEER_HEADS):
                n1, f1 = bcast[hh]
                gate = gate + jnp.where(r2_ref[hh, rs, :] < n1, f2_ref[hh, rs, :], jnp.zeros((), BF16)) * f1
            o0 = r * nkeys + g * sub
            ga_ref[o0:o0 + sub, :] = _gelu_times_gate(pre[o0:o0 + sub, :], gate)
    acc_ref[...] += _dot(pltpu.bitcast(vt_ref[...], BF16), ga_ref[...])

    @pl.when(e == pl.num_programs(1) - 1)
    def _():
        o_ref[...] = h_ref[...] + acc_ref[...].T


def _peer_experts(h2, hnt, r2, f2, n1, f1, expert_u, expert_v, tile, rows_per_step):
    n, d = h2.shape
    heads, nkeys, _ = r2.shape
    n_exp = expert_u.shape[0]
    te = rows_per_step * nkeys
    assert n_exp == nkeys * nkeys and n_exp % te == 0 and rows_per_step % 8 == 0
    u, vt = _pack_expert_weights(expert_u, expert_v, rows=512)
    kern = functools.partial(_peer_expert_kernel, tile=tile, nkeys=nkeys, rows_per_step=rows_per_step)
    tab2 = pl.BlockSpec((heads, nkeys, tile), lambda t, e: (0, 0, t))
    tab1 = pl.BlockSpec((heads, rows_per_step, tile), lambda t, e: (0, e, t))
    return pl.pallas_call(
        kern,
        grid=(n // tile, n_exp // te),
        in_specs=[
            pl.BlockSpec((d, tile), lambda t, e: (0, t)),
            pl.BlockSpec((te // 2, d), lambda t, e: (e, 0)),
            pl.BlockSpec((d // 2, te), lambda t, e: (0, e)),
            tab2, tab2, tab1, tab1,
            pl.BlockSpec((tile, d), lambda t, e: (t, 0)),
        ],
        out_specs=pl.BlockSpec((tile, d), lambda t, e: (t, 0)),
        out_shape=jax.ShapeDtypeStruct((n, d), F32),
        scratch_shapes=[pltpu.VMEM((d, tile), F32), pltpu.VMEM((te, tile), BF16)],
        compiler_params=_cparams(("arbitrary", "arbitrary")),
        name="peer_experts",
    )(hnt, u, vt, r2, f2, n1, f1, h2)


def _peer_ffn(h2, norm_g, w_query, sub_keys, expert_u, expert_v, route_tile, expert_tile):
    hnt, r2, f2, n1, f1 = _peer_route(h2, norm_g, w_query, sub_keys, route_tile)
    return _peer_experts(h2, hnt, r2, f2, n1, f1, expert_u, expert_v, expert_tile, rows_per_step=16)


def _pick_tile(n, candidates):
    for c in candidates:
        if n % c == 0:
            return c
    raise ValueError(f"no tile in {candidates} divides {n}")


def kernel(x, meta_tokens, even_norm_g, even_w_in, even_q_norm_g, even_k_norm_g, even_conv_w, even_w_out,
           odd_norm_g, odd_pool_w, odd_pool_scale, ffn_norm_g, peer_w_query, peer_sub_keys, peer_u, peer_v):
    b, seq, d = x.shape
    total = N_META + seq
    lp = -(-total // SEQ_ALIGN) * SEQ_ALIGN
    meta = jnp.broadcast_to(meta_tokens[None].astype(x.dtype), (b, N_META, d))
    h = jnp.concatenate([meta, x], axis=1)
    h = jnp.pad(h, ((0, 0), (0, lp - total), (0, 0)))
    n = b * lp
    seq_tile = _pick_tile(lp, (384, 256, 128))
    flat_tile = _pick_tile(n, (768, 512, 384, 256, 128))
    route_tile = _pick_tile(n, (256, 128))
    depth = ffn_norm_g.shape[0]
    head_dim = even_q_norm_g.shape[1]
    for layer in range(depth):
        i = layer // 2
        if layer % 2 == 0:
            q, kt, v, conv = _even_front(h, even_norm_g[i], even_w_in[i], even_q_norm_g[i], even_k_norm_g[i],
                                         even_conv_w[i], seq_tile)
            att = _stickbreak_attention(q, kt, v, head_dim)
            width = att.shape[-1]
            h2 = _even_out(h.reshape(n, d), att.reshape(n, width), conv.reshape(n, width), even_w_out[i], flat_tile)
        else:
            h2 = _odd_layer(h, odd_norm_g[i], odd_pool_w[i], odd_pool_scale[i], seq_tile).reshape(n, d)
        h2 = _peer_ffn(h2, ffn_norm_g[layer], peer_w_query[layer], peer_sub_keys[layer], peer_u[layer], peer_v[layer],
                       route_tile, flat_tile)
        h = h2.reshape(b, lp, d)
    return h[:, N_META:N_META + seq]
```

```python
import functools
import math

import jax
import jax.numpy as jnp
from jax import lax
from jax.experimental import pallas as pl
from jax.experimental.pallas import tpu as pltpu

F32 = jnp.float32
BF16 = jnp.bfloat16

EPS = 1e-6
N_META = 16
SEQ_ALIGN = 128
POOL_WINDOWS = (2, 4, 8, 16)
CONV_K = 3
SB_HEADS = 8
PEER_HEADS = 8
PEER_TOPK = 16

LANES = 128
SUBLANES = 8
NEG_INF = float("-inf")
SB_DEAD_LOG = -105.0

VMEM_LIMIT = 56 * 1024 * 1024


def _cparams(sem):
    return pltpu.CompilerParams(dimension_semantics=sem, vmem_limit_bytes=VMEM_LIMIT)


def _split_bf16(x):
    hi = x.astype(BF16)
    lo = (x - hi.astype(F32)).astype(BF16)
    return hi, lo


def _dot(a, b):
    return jnp.dot(a, b, preferred_element_type=F32)


def _rms_rows(x, g):
    ms = jnp.mean(x * x, axis=-1, keepdims=True)
    return x * lax.rsqrt(ms + EPS) * g


def _even_front_kernel(h_ref, g_ref, win_ref, qg_ref, kg_ref, cw_ref, bd_ref,
                       q_ref, kt_ref, v_ref, conv_ref, cbuf, *, tile, width, head_dim):
    i = pl.program_id(1)
    hn = _rms_rows(h_ref[...], g_ref[...])
    proj = _dot(hn.astype(BF16), win_ref[...])
    bd = bd_ref[...]

    def head_norm(t, gain):
        hi, lo = _split_bf16(t * t)
        ssq = _dot(hi, bd) + _dot(lo, bd)
        return t * lax.rsqrt(ssq * (1.0 / head_dim) + EPS) * gain

    q = head_norm(proj[:, 0:width], qg_ref[...]) * (head_dim ** -0.5)
    k = head_norm(proj[:, width:2 * width], kg_ref[...])
    q_ref[...] = q.astype(BF16)
    kt_ref[...] = k.T.astype(BF16)
    v_ref[...] = proj[:, 2 * width:3 * width].astype(BF16)

    gate_b = proj[:, 3 * width:4 * width]
    cu = proj[:, 4 * width:5 * width] * proj[:, 5 * width:6 * width]

    @pl.when(i == 0)
    def _():
        cbuf[0:8, :] = jnp.zeros((8, width), F32)

    cbuf[8:8 + tile, :] = cu
    y = cw_ref[2:3, :] * cu
    for s in range(1, CONV_K):
        y = y + cw_ref[CONV_K - 1 - s:CONV_K - s, :] * cbuf[8 - s:8 - s + tile, :]
    conv_ref[...] = (gate_b * y).astype(BF16)
    cbuf[0:8, :] = cu[tile - 8:tile, :]


def _even_front(h, norm_g, w_in, q_g, k_g, conv_w, tile):
    b, lp, d = h.shape
    width = w_in.shape[1] // 6
    head_dim = q_g.shape[0]
    heads = width // head_dim
    hid = jnp.arange(width) // head_dim
    bd = (hid[:, None] == hid[None, :]).astype(BF16)
    kern = functools.partial(_even_front_kernel, tile=tile, width=width, head_dim=head_dim)
    out_shapes = (
        jax.ShapeDtypeStruct((b, lp, width), BF16),
        jax.ShapeDtypeStruct((b, width, lp), BF16),
        jax.ShapeDtypeStruct((b, lp, width), BF16),
        jax.ShapeDtypeStruct((b, lp, width), BF16),
    )
    row_spec = pl.BlockSpec((None, tile, width), lambda bi, i: (bi, i, 0))
    full = lambda shape: pl.BlockSpec(shape, lambda bi, i: (0,) * len(shape))
    return pl.pallas_call(
        kern,
        grid=(b, lp // tile),
        in_specs=[
            pl.BlockSpec((None, tile, d), lambda bi, i: (bi, i, 0)),
            full((1, d)), full((d, 6 * width)), full((1, width)), full((1, width)),
            full((CONV_K, width)), full((width, width)),
        ],
        out_specs=(row_spec, pl.BlockSpec((None, width, tile), lambda bi, i: (bi, 0, i)), row_spec, row_spec),
        out_shape=out_shapes,
        scratch_shapes=[pltpu.VMEM((tile + 8, width), F32)],
        compiler_params=_cparams(("arbitrary", "arbitrary")),
        name="even_front",
    )(h, norm_g.reshape(1, d), w_in.astype(BF16), jnp.tile(q_g, heads).reshape(1, width),
      jnp.tile(k_g, heads).reshape(1, width), conv_w, bd)


def _stickbreak_kernel(q_ref, kt_ref, v_ref, uj_ref, o_ref, carry_ref, acc_ref, *, blk, head_dim, pairs):
    i = pl.program_id(1)
    pair = 2 * head_dim
    lane = lax.broadcasted_iota(jnp.int32, (blk, pair), 1)
    row = lax.broadcasted_iota(jnp.int32, (blk, blk), 0)
    col = lax.broadcasted_iota(jnp.int32, (blk, blk), 1)
    uj = uj_ref[...]

    carry_ref[...] = jnp.zeros(carry_ref.shape, F32)
    acc_ref[...] = jnp.zeros(acc_ref.shape, F32)

    def sweep_block(j):
        start = pl.multiple_of(j * blk, blk)
        causal = (j * blk + col) < (i * blk + row)
        heads = range(2 * pairs)
        zs = []
        for p in range(pairs):
            ps = slice(p * pair, (p + 1) * pair)
            q = q_ref[:, ps]
            kt = kt_ref[ps, pl.ds(start, blk)]
            zero = jnp.zeros_like(q)
            zs.append(_dot(jnp.where(lane < head_dim, q, zero), kt))
            zs.append(_dot(jnp.where(lane >= head_dim, q, zero), kt))
        cums, log_betas = [], []
        for hd in heads:
            z = zs[hd]
            l1p = jnp.log(1.0 + jnp.exp(-jnp.abs(z)))
            log_keep = jnp.where(causal, -(jnp.maximum(z, 0.0) + l1p), 0.0)
            log_betas.append(jnp.minimum(z, 0.0) - l1p)
            cums.append(_dot(log_keep.astype(BF16), uj))
        top = None
        for hd in heads:
            p = hd // 2
            vb = v_ref[pl.ds(start, blk), p * pair:(p + 1) * pair]
            carry = carry_ref[hd]
            later = cums[hd][:, :blk] + carry
            w = jnp.where(causal, jnp.exp(log_betas[hd] + later), 0.0)
            acc_ref[hd] += _dot(w.astype(BF16), vb)
            carry = carry + cums[hd][:, blk:]
            carry_ref[hd] = carry
            m = jnp.max(carry, axis=0, keepdims=True)
            top = m if top is None else jnp.maximum(top, m)
        return jnp.max(top)

    top0 = sweep_block(i)

    def cond(state):
        j, top = state
        return jnp.logical_and(j >= 0, top > SB_DEAD_LOG)

    def body(state):
        j, _ = state
        return j - 1, sweep_block(j)

    lax.while_loop(cond, body, (i - 1, top0))
    for p in range(pairs):
        o_ref[:, p * pair:(p + 1) * pair] = jnp.where(lane < head_dim, acc_ref[2 * p], acc_ref[2 * p + 1]).astype(BF16)


def _stickbreak_attention(q, kt, v, head_dim):
    b, lp, width = q.shape
    blk = SEQ_ALIGN
    pair = 2 * head_dim
    assert pair == LANES and width % pair == 0 and lp % blk == 0
    pairs = width // pair
    r = jnp.arange(blk)
    later_mat = (r[:, None] > r[None, :]).astype(BF16)
    uj = jnp.concatenate([later_mat, jnp.ones((blk, blk), BF16)], axis=1)
    kern = functools.partial(_stickbreak_kernel, blk=blk, head_dim=head_dim, pairs=pairs)
    resident = pl.Buffered(1)
    return pl.pallas_call(
        kern,
        grid=(b, lp // blk),
        in_specs=[
            pl.BlockSpec((None, blk, width), lambda bi, i: (bi, i, 0)),
            pl.BlockSpec((None, width, lp), lambda bi, i: (bi, 0, 0), pipeline_mode=resident),
            pl.BlockSpec((None, lp, width), lambda bi, i: (bi, 0, 0), pipeline_mode=resident),
            pl.BlockSpec((blk, 2 * blk), lambda bi, i: (0, 0)),
        ],
        out_specs=pl.BlockSpec((None, blk, width), lambda bi, i: (bi, i, 0)),
        out_shape=jax.ShapeDtypeStruct((b, lp, width), BF16),
        scratch_shapes=[pltpu.VMEM((2 * pairs, blk, blk), F32), pltpu.VMEM((2 * pairs, blk, pair), F32)],
        compiler_params=_cparams(("arbitrary", "arbitrary")),
        name="stickbreak_attention",
    )(q, kt, v, uj)


def _even_out_kernel(h_ref, att_ref, conv_ref, wa_ref, wc_ref, o_ref):
    o_ref[...] = h_ref[...] + _dot(att_ref[...], wa_ref[...]) + _dot(conv_ref[...], wc_ref[...])


def _even_out(h2, att2, conv2, w_out, tile):
    n, d = h2.shape
    width = att2.shape[1]
    w = w_out.astype(BF16)
    return pl.pallas_call(
        _even_out_kernel,
        grid=(n // tile,),
        in_specs=[
            pl.BlockSpec((tile, d), lambda i: (i, 0)),
            pl.BlockSpec((tile, width), lambda i: (i, 0)),
            pl.BlockSpec((tile, width), lambda i: (i, 0)),
            pl.BlockSpec((width, d), lambda i: (0, 0)),
            pl.BlockSpec((width, d), lambda i: (0, 0)),
        ],
        out_specs=pl.BlockSpec((tile, d), lambda i: (i, 0)),
        out_shape=jax.ShapeDtypeStruct((n, d), F32),
        compiler_params=_cparams(("arbitrary",)),
        name="even_out",
    )(h2, att2, conv2, w[:width], w[width:])


def _odd_kernel(h_ref, g_ref, pw_ref, ps_ref, o_ref, buf, *, tile, group):
    i = pl.program_id(1)
    halo = max(POOL_WINDOWS)
    x = h_ref[...]
    hn = _rms_rows(x, g_ref[...])

    @pl.when(i == 0)
    def _():
        buf[0:halo, :] = jnp.zeros((halo, hn.shape[1]), F32)

    buf[halo:halo + tile, :] = hn
    pos = i * tile + lax.broadcasted_iota(jnp.int32, (tile, group), 0)
    ys = []
    for gi, w in enumerate(POOL_WINDOWS):
        c0 = gi * group
        cur = hn[:, c0:c0 + group]
        s = cur
        for sft in range(1, w):
            s = s + buf[halo - sft:halo - sft + tile, c0:c0 + group]
        count = jnp.minimum(pos + 1, w).astype(F32)
        diff = s / count - cur
        ys.append(_dot(diff.astype(BF16), pw_ref[gi]))
    y = jnp.concatenate(ys, axis=1)
    o_ref[...] = x + y * ps_ref[...]
    buf[0:halo, :] = hn[tile - halo:tile, :]


def _odd_layer(h, norm_g, pool_w, pool_scale, tile):
    b, lp, d = h.shape
    groups, group, _ = pool_w.shape
    assert groups == len(POOL_WINDOWS) and groups * group == d
    kern = functools.partial(_odd_kernel, tile=tile, group=group)
    return pl.pallas_call(
        kern,
        grid=(b, lp // tile),
        in_specs=[
            pl.BlockSpec((None, tile, d), lambda bi, i: (bi, i, 0)),
            pl.BlockSpec((1, d), lambda bi, i: (0, 0)),
            pl.BlockSpec((groups, group, group), lambda bi, i: (0, 0, 0)),
            pl.BlockSpec((1, d), lambda bi, i: (0, 0)),
        ],
        out_specs=pl.BlockSpec((None, tile, d), lambda bi, i: (bi, i, 0)),
        out_shape=jax.ShapeDtypeStruct((b, lp, d), F32),
        scratch_shapes=[pltpu.VMEM((tile + max(POOL_WINDOWS), d), F32)],
        compiler_params=_cparams(("arbitrary", "arbitrary")),
        name="odd_layer",
    )(h, norm_g.reshape(1, d), pool_w.astype(BF16), pool_scale.reshape(1, d))


def _fold_leading(x, op):
    parts = [x[i] for i in range(x.shape[0])]
    while len(parts) > 1:
        parts = [op(parts[i], parts[i + 1]) for i in range(0, len(parts) - 1, 2)] + parts[len(parts) & ~1:]
    return parts[0]


def _fold_rows(x, op):
    for shift in (4, 2, 1):
        x = op(x, pltpu.roll(x, shift, 0))
    return x


def _top16_rows(s, exact_ties, want_rank):
    groups = s.shape[0]
    if exact_ties:
        key_id = (lax.broadcasted_iota(jnp.int32, s.shape, 0) * SUBLANES
                  + lax.broadcasted_iota(jnp.int32, s.shape, 1)).astype(F32)
    rank = jnp.full(s.shape, float(PEER_TOPK), F32) if want_rank else None
    vals = []
    for it in range(PEER_TOPK):
        m = _fold_rows(_fold_leading(s, jnp.maximum), jnp.maximum)
        hit = s == m[None]
        if exact_ties:
            first = _fold_leading(jnp.where(hit, key_id, float(groups * SUBLANES)), jnp.minimum)
            hit = key_id == _fold_rows(first, jnp.minimum)[None]
        if want_rank:
            rank = jnp.where(hit, float(it), rank)
        s = jnp.where(hit, NEG_INF, s)
        vals.append(m)
    return rank, vals, s


def _pair_partner_counts(v1, v2, exact_ties):
    k = PEER_TOPK
    half = k // 2
    assert half == SUBLANES
    sub = lax.broadcasted_iota(jnp.int32, (SUBLANES, LANES), 0)

    def as_rows(vals):
        out = vals[0]
        for r in range(1, SUBLANES):
            out = jnp.where(sub == r, vals[r], out)
        return out

    v2_lo, v2_hi = as_rows(v2[:half]), as_rows(v2[half:])
    tiles = [v1[0] + v2_lo, v1[0] + v2_hi]
    flat_base = [0, half]
    masked = [0, 0]
    for a in range(1, half):
        bmax = k // (a + 1) - 1
        tiles.append(jnp.where(sub <= bmax, v1[a] + v2_lo, NEG_INF))
        flat_base.append(a * k)
        masked.append(half - 1 - bmax)
    tiles.append(as_rows(v1[half:]) + v2[0])
    if exact_ties:
        flats = [(sub + base).astype(F32) for base in flat_base] + [((sub + half) * k).astype(F32)]

    top = v1[0] + v2[0]
    denom = jnp.zeros((SUBLANES, LANES), F32)
    big = float(k * k)
    for _ in range(k):
        m = tiles[0]
        for t in tiles[1:]:
            m = jnp.maximum(m, t)
        m = _fold_rows(m, jnp.maximum)
        if exact_ties:
            first = None
            for t, f in zip(tiles, flats):
                c = jnp.where(t == m, f, big)
                first = c if first is None else jnp.minimum(first, c)
            first = _fold_rows(first, jnp.minimum)
            tiles = [jnp.where(f == first, NEG_INF, t) for t, f in zip(tiles, flats)]
        else:
            tiles = [jnp.where(t == m, NEG_INF, t) for t in tiles]
        denom = denom + jnp.exp(m - top)
    gone = [jnp.where(t == NEG_INF, 1.0, 0.0) for t in tiles]
    counts = [_fold_rows(gone[0] + gone[1], jnp.add)]
    counts += [_fold_rows(gone[a + 1], jnp.add) - float(masked[a + 1]) for a in range(1, half)]
    counts += [_fold_rows(jnp.where(sub == r, gone[-1], 0.0), jnp.add) for r in range(half)]
    return counts, denom


def _route_chunk(s1, s2, exact_ties):
    k = float(PEER_TOPK)
    rank1, v1, left1 = _top16_rows(s1, exact_ties, want_rank=exact_ties)
    rank2, v2, left2 = _top16_rows(s2, exact_ties, want_rank=True)
    counts, denom = _pair_partner_counts(v1, v2, exact_ties)
    n1 = jnp.zeros(s1.shape, F32)
    for a in range(PEER_TOPK):
        is_a = (rank1 == float(a)) if exact_ties else (s1 == v1[a][None])
        n1 = jnp.where(is_a, counts[a][None], n1)
    f1 = jnp.exp(s1 - v1[0][None])
    f2 = jnp.exp(s2 - v2[0][None]) / denom[None]
    if exact_ties:
        bad = jnp.zeros((SUBLANES, LANES), F32)
    else:
        def picked(left):
            return _fold_rows(_fold_leading(jnp.where(left == NEG_INF, 1.0, 0.0), jnp.add), jnp.add)

        pairs = counts[0]
        for c in counts[1:]:
            pairs = pairs + c
        ok = jnp.logical_and(jnp.logical_and(picked(left1) == k, picked(left2) == k), pairs == k)
        bad = jnp.where(ok, 0.0, 1.0)
    return n1, f1, rank2, f2, bad


def _peer_route_kernel(h_ref, g_ref, kwh_ref, kwl_ref,
                       hnt_ref, r2_ref, f2_ref, n1_ref, f1_ref, hi_scr, lo_scr, s_scr, *, tile, nkeys):
    hn = _rms_rows(h_ref[...], g_ref[...])
    hn_hi, hn_lo = _split_bf16(hn.T)
    hnt_ref[...] = hn_hi
    hi_scr[...] = hn_hi
    lo_scr[...] = hn_lo
    chunks = [slice(c * LANES, (c + 1) * LANES) for c in range(tile // LANES)]

    def head_scores(hh, slot):
        rows = pl.ds(pl.multiple_of(hh * 2 * nkeys, 2 * nkeys), 2 * nkeys)
        kwh = kwh_ref[rows, :]
        hi = hi_scr[...]
        s_scr[slot] = _dot(kwh, hi) + _dot(kwh, lo_scr[...]) + _dot(kwl_ref[rows, :], hi)

    head_scores(0, 0)

    def per_head(hh, _):
        slot = lax.rem(hh, 2)
        s1 = s_scr[slot, 0:nkeys, :]
        s2 = s_scr[slot, nkeys:2 * nkeys, :]
        head_scores(jnp.minimum(hh + 1, PEER_HEADS - 1), 1 - slot)

        def keys_by_tile(x):
            return x.reshape(nkeys // SUBLANES, SUBLANES, LANES)

        def write(cs, n1, f1, rank2, f2):
            n1_ref[hh, :, cs] = n1.reshape(nkeys, LANES)
            f1_ref[hh, :, cs] = f1.reshape(nkeys, LANES)
            r2_ref[hh, :, cs] = rank2.reshape(nkeys, LANES).astype(BF16)
            f2_ref[hh, :, cs] = f2.reshape(nkeys, LANES).astype(BF16)

        def route(cs, exact_ties):
            return _route_chunk(keys_by_tile(s1[:, cs]), keys_by_tile(s2[:, cs]), exact_ties)

        bad = None
        for cs in chunks:
            n1, f1, rank2, f2, bad_c = route(cs, exact_ties=False)
            write(cs, n1, f1, rank2, f2)
            bad = bad_c if bad is None else jnp.maximum(bad, bad_c)

        @pl.when(jnp.max(bad) > 0.0)
        def _():
            for cs in chunks:
                write(cs, *route(cs, exact_ties=True)[:4])
        return 0

    lax.fori_loop(0, PEER_HEADS, per_head, 0)


def _split3_bf16(x):
    hi = x.astype(BF16)
    r1 = x - hi.astype(F32)
    mid = r1.astype(BF16)
    lo = (r1 - mid.astype(F32)).astype(BF16)
    return hi, mid, lo


def _fold_keys_kernel(k_ref, wq_ref, o_ref):
    kh, km, kl = _split3_bf16(k_ref[...])
    wh, wm, wl = _split3_bf16(wq_ref[...])
    o_ref[...] = (_dot(kh, wh) + (_dot(kh, wm) + _dot(km, wh))
                  + (_dot(km, wm) + _dot(kh, wl) + _dot(kl, wh)))


def _fold_keys(w_query, sub_keys):
    d = w_query.shape[0]
    _, heads, nkeys, half = sub_keys.shape
    keys = sub_keys.transpose(1, 0, 2, 3).reshape(heads * 2, nkeys, half)
    return pl.pallas_call(
        _fold_keys_kernel,
        grid=(heads * 2,),
        in_specs=[pl.BlockSpec((None, nkeys, half), lambda r: (r, 0, 0)),
                  pl.BlockSpec((half, d), lambda r: (r, 0))],
        out_specs=pl.BlockSpec((nkeys, d), lambda r: (r, 0)),
        out_shape=jax.ShapeDtypeStruct((heads * 2 * nkeys, d), F32),
        compiler_params=_cparams(("arbitrary",)),
        name="peer_fold_keys",
    )(keys, w_query.T)


def _peer_route(h2, norm_g, w_query, sub_keys, tile):
    n, d = h2.shape
    _, heads, nkeys, half = sub_keys.shape
    assert heads == PEER_HEADS and nkeys == half == LANES
    srows = heads * 2 * nkeys
    kw_hi, kw_lo = _split_bf16(_fold_keys(w_query, sub_keys))
    kern = functools.partial(_peer_route_kernel, tile=tile, nkeys=nkeys)
    tab = lambda dt: jax.ShapeDtypeStruct((heads, nkeys, n), dt)
    tab_spec = pl.BlockSpec((heads, nkeys, tile), lambda i: (0, 0, i))
    const = lambda shape: pl.BlockSpec(shape, lambda i: (0,) * len(shape))
    return pl.pallas_call(
        kern,
        grid=(n // tile,),
        in_specs=[
            pl.BlockSpec((tile, d), lambda i: (i, 0)),
            const((1, d)), const((srows, d)), const((srows, d)),
        ],
        out_specs=(pl.BlockSpec((d, tile), lambda i: (0, i)), tab_spec, tab_spec, tab_spec, tab_spec),
        out_shape=(jax.ShapeDtypeStruct((d, n), BF16), tab(BF16), tab(BF16), tab(F32), tab(F32)),
        scratch_shapes=[pltpu.VMEM((d, tile), BF16), pltpu.VMEM((d, tile), BF16),
                        pltpu.VMEM((2, 2 * nkeys, tile), F32)],
        compiler_params=_cparams(("arbitrary",)),
        name="peer_route",
    )(h2, norm_g.reshape(1, d), kw_hi, kw_lo)


def _gelu_times_gate(a, gate):
    k = -2.0 * math.sqrt(2.0 / math.pi) * math.log2(math.e)
    u = a * ((a * a) * (k * 0.044715) + k)
    return a.astype(BF16) * gate / (1.0 + jnp.exp2(u.astype(BF16)))


def _pack_rows_kernel(x_ref, o_ref):
    o_ref[...] = pltpu.bitcast(x_ref[...].astype(BF16), jnp.uint32)


def _pack_cols_kernel(x_ref, o_ref):
    o_ref[...] = pltpu.bitcast(x_ref[...].T.astype(BF16), jnp.uint32)


def _pack_expert_weights(expert_u, expert_v, layer, rows):
    _, n_exp, d = expert_u.shape
    grid = (n_exp // rows,)
    in_spec = pl.BlockSpec((None, rows, d), lambda i: (layer, i, 0))
    u = pl.pallas_call(
        _pack_rows_kernel, grid=grid, in_specs=[in_spec],
        out_specs=pl.BlockSpec((rows // 2, d), lambda i: (i, 0)),
        out_shape=jax.ShapeDtypeStruct((n_exp // 2, d), jnp.uint32),
        compiler_params=_cparams(("arbitrary",)), name="peer_pack_u",
    )(expert_u)
    vt = pl.pallas_call(
        _pack_cols_kernel, grid=grid, in_specs=[in_spec],
        out_specs=pl.BlockSpec((d // 2, rows), lambda i: (0, i)),
        out_shape=jax.ShapeDtypeStruct((d // 2, n_exp), jnp.uint32),
        compiler_params=_cparams(("arbitrary",)), name="peer_pack_vt",
    )(expert_v)
    return u, vt


def _peer_expert_kernel(hnt_ref, u_ref, vt_ref, r2_ref, f2_ref, n1_ref, f1_ref, h_ref,
                        o_ref, acc_ref, ga_ref, *, tile, nkeys, rows_per_step):
    e = pl.program_id(1)

    @pl.when(e == 0)
    def _():
        acc_ref[...] = jnp.zeros(acc_ref.shape, F32)

    pre = _dot(pltpu.bitcast(u_ref[...], BF16), hnt_ref[...])
    sub = 16
    for r in range(rows_per_step):
        bcast = []
        for hh in range(PEER_HEADS):
            n1 = jnp.broadcast_to(n1_ref[hh, r:r + 1, :], (sub, tile)).astype(BF16)
            f1 = jnp.broadcast_to(f1_ref[hh, r:r + 1, :], (sub, tile)).astype(BF16)
            bcast.append((n1, f1))
        for g in range(nkeys // sub):
            rs = slice(g * sub, (g + 1) * sub)
            gate = jnp.zeros((sub, tile), BF16)
            for hh in range(PEER_HEADS):
                n1, f1 = bcast[hh]
                gate = gate + jnp.where(r2_ref[hh, rs, :] < n1, f2_ref[hh, rs, :], jnp.zeros((), BF16)) * f1
            o0 = r * nkeys + g * sub
            ga_ref[o0:o0 + sub, :] = _gelu_times_gate(pre[o0:o0 + sub, :], gate)
    acc_ref[...] += _dot(pltpu.bitcast(vt_ref[...], BF16), ga_ref[...])

    @pl.when(e == pl.num_programs(1) - 1)
    def _():
        o_ref[...] = h_ref[...] + acc_ref[...].T


def _peer_experts(h2, hnt, r2, f2, n1, f1, expert_u, expert_v, layer, tile, rows_per_step):
    n, d = h2.shape
    heads, nkeys, _ = r2.shape
    n_exp = expert_u.shape[1]
    te = rows_per_step * nkeys
    assert n_exp == nkeys * nkeys and n_exp % te == 0 and rows_per_step % 8 == 0
    u, vt = _pack_expert_weights(expert_u, expert_v, layer, rows=512)
    kern = functools.partial(_peer_expert_kernel, tile=tile, nkeys=nkeys, rows_per_step=rows_per_step)
    tab2 = pl.BlockSpec((heads, nkeys, tile), lambda t, e: (0, 0, t))
    tab1 = pl.BlockSpec((heads, rows_per_step, tile), lambda t, e: (0, e, t))
    return pl.pallas_call(
        kern,
        grid=(n // tile, n_exp // te),
        in_specs=[
            pl.BlockSpec((d, tile), lambda t, e: (0, t)),
            pl.BlockSpec((te // 2, d), lambda t, e: (e, 0)),
            pl.BlockSpec((d // 2, te), lambda t, e: (0, e)),
            tab2, tab2, tab1, tab1,
            pl.BlockSpec((tile, d), lambda t, e: (t, 0)),
        ],
        out_specs=pl.BlockSpec((tile, d), lambda t, e: (t, 0)),
        out_shape=jax.ShapeDtypeStruct((n, d), F32),
        scratch_shapes=[pltpu.VMEM((d, tile), F32), pltpu.VMEM((te, tile), BF16)],
        compiler_params=_cparams(("arbitrary", "arbitrary")),
        name="peer_experts",
    )(hnt, u, vt, r2, f2, n1, f1, h2)


def _peer_ffn(h2, norm_g, w_query, sub_keys, expert_u, expert_v, layer, route_tile, expert_tile):
    hnt, r2, f2, n1, f1 = _peer_route(h2, norm_g, w_query, sub_keys, route_tile)
    return _peer_experts(h2, hnt, r2, f2, n1, f1, expert_u, expert_v, layer, expert_tile, rows_per_step=16)


def _pick_tile(n, candidates):
    for c in candidates:
        if n % c == 0:
            return c
    raise ValueError(f"no tile in {candidates} divides {n}")


def kernel(x, meta_tokens, even_norm_g, even_w_in, even_q_norm_g, even_k_norm_g, even_conv_w, even_w_out,
           odd_norm_g, odd_pool_w, odd_pool_scale, ffn_norm_g, peer_w_query, peer_sub_keys, peer_u, peer_v):
    b, seq, d = x.shape
    total = N_META + seq
    lp = -(-total // SEQ_ALIGN) * SEQ_ALIGN
    meta = jnp.broadcast_to(meta_tokens[None].astype(x.dtype), (b, N_META, d))
    h = jnp.concatenate([meta, x], axis=1)
    h = jnp.pad(h, ((0, 0), (0, lp - total), (0, 0)))
    n = b * lp
    seq_tile = _pick_tile(lp, (384, 256, 128))
    flat_tile = _pick_tile(n, (768, 512, 384, 256, 128))
    route_tile = _pick_tile(n, (256, 128))
    depth = ffn_norm_g.shape[0]
    head_dim = even_q_norm_g.shape[1]
    for layer in range(depth):
        i = layer // 2
        if layer % 2 == 0:
            q, kt, v, conv = _even_front(h, even_norm_g[i], even_w_in[i], even_q_norm_g[i], even_k_norm_g[i],
                                         even_conv_w[i], seq_tile)
            att = _stickbreak_attention(q, kt, v, head_dim)
            width = att.shape[-1]
            h2 = _even_out(h.reshape(n, d), att.reshape(n, width), conv.reshape(n, width), even_w_out[i], flat_tile)
        else:
            h2 = _odd_layer(h, odd_norm_g[i], odd_pool_w[i], odd_pool_scale[i], seq_tile).reshape(n, d)
        h2 = _peer_ffn(h2, ffn_norm_g[layer], peer_w_query[layer], peer_sub_keys[layer], peer_u, peer_v, layer,
                       route_tile, flat_tile)
        h = h2.reshape(b, lp, d)
    return h[:, N_META:N_META + seq]
```

```python
import functools
import math

import jax
import jax.numpy as jnp
from jax import lax
from jax.experimental import pallas as pl
from jax.experimental.pallas import tpu as pltpu

F32 = jnp.float32
BF16 = jnp.bfloat16

EPS = 1e-6
N_META = 16
SEQ_ALIGN = 128
POOL_WINDOWS = (2, 4, 8, 16)
CONV_K = 3
SB_HEADS = 8
PEER_HEADS = 8
PEER_TOPK = 16

LANES = 128
SUBLANES = 8
NEG_INF = float("-inf")
SB_DEAD_LOG = -105.0

VMEM_LIMIT = 56 * 1024 * 1024


def _cparams(sem):
    return pltpu.CompilerParams(dimension_semantics=sem, vmem_limit_bytes=VMEM_LIMIT)


def _split_bf16(x):
    hi = x.astype(BF16)
    lo = (x - hi.astype(F32)).astype(BF16)
    return hi, lo


def _dot(a, b):
    return jnp.dot(a, b, preferred_element_type=F32)


def _rms_rows(x, g):
    ms = jnp.mean(x * x, axis=-1, keepdims=True)
    return x * lax.rsqrt(ms + EPS) * g


def _even_front_kernel(h_ref, g_ref, win_ref, qg_ref, kg_ref, cw_ref, bd_ref,
                       q_ref, kt_ref, v_ref, conv_ref, cbuf, *, tile, width, head_dim):
    i = pl.program_id(1)
    hn = _rms_rows(h_ref[...], g_ref[...])
    proj = _dot(hn.astype(BF16), win_ref[...])
    bd = bd_ref[...]

    def head_norm(t, gain):
        hi, lo = _split_bf16(t * t)
        ssq = _dot(hi, bd) + _dot(lo, bd)
        return t * lax.rsqrt(ssq * (1.0 / head_dim) + EPS) * gain

    q = head_norm(proj[:, 0:width], qg_ref[...]) * (head_dim ** -0.5)
    k = head_norm(proj[:, width:2 * width], kg_ref[...])
    q_ref[...] = q.astype(BF16)
    kt_ref[...] = k.T.astype(BF16)
    v_ref[...] = proj[:, 2 * width:3 * width].astype(BF16)

    gate_b = proj[:, 3 * width:4 * width]
    cu = proj[:, 4 * width:5 * width] * proj[:, 5 * width:6 * width]

    @pl.when(i == 0)
    def _():
        cbuf[0:8, :] = jnp.zeros((8, width), F32)

    cbuf[8:8 + tile, :] = cu
    y = cw_ref[2:3, :] * cu
    for s in range(1, CONV_K):
        y = y + cw_ref[CONV_K - 1 - s:CONV_K - s, :] * cbuf[8 - s:8 - s + tile, :]
    conv_ref[...] = (gate_b * y).astype(BF16)
    cbuf[0:8, :] = cu[tile - 8:tile, :]


def _even_front(h, norm_g, w_in, q_g, k_g, conv_w, tile):
    b, lp, d = h.shape
    width = w_in.shape[1] // 6
    head_dim = q_g.shape[0]
    heads = width // head_dim
    hid = jnp.arange(width) // head_dim
    bd = (hid[:, None] == hid[None, :]).astype(BF16)
    kern = functools.partial(_even_front_kernel, tile=tile, width=width, head_dim=head_dim)
    out_shapes = (
        jax.ShapeDtypeStruct((b, lp, width), BF16),
        jax.ShapeDtypeStruct((b, width, lp), BF16),
        jax.ShapeDtypeStruct((b, lp, width), BF16),
        jax.ShapeDtypeStruct((b, lp, width), BF16),
    )
    row_spec = pl.BlockSpec((None, tile, width), lambda bi, i: (bi, i, 0))
    full = lambda shape: pl.BlockSpec(shape, lambda bi, i: (0,) * len(shape))
    return pl.pallas_call(
        kern,
        grid=(b, lp // tile),
        in_specs=[
            pl.BlockSpec((None, tile, d), lambda bi, i: (bi, i, 0)),
            full((1, d)), full((d, 6 * width)), full((1, width)), full((1, width)),
            full((CONV_K, width)), full((width, width)),
        ],
        out_specs=(row_spec, pl.BlockSpec((None, width, tile), lambda bi, i: (bi, 0, i)), row_spec, row_spec),
        out_shape=out_shapes,
        scratch_shapes=[pltpu.VMEM((tile + 8, width), F32)],
        compiler_params=_cparams(("arbitrary", "arbitrary")),
        name="even_front",
    )(h, norm_g.reshape(1, d), w_in.astype(BF16), jnp.tile(q_g, heads).reshape(1, width),
      jnp.tile(k_g, heads).reshape(1, width), conv_w, bd)


def _stickbreak_kernel(q_ref, kt_ref, v_ref, uj_ref, o_ref, carry_ref, acc_ref, *, blk, head_dim, pairs):
    i = pl.program_id(1)
    pair = 2 * head_dim
    lane = lax.broadcasted_iota(jnp.int32, (blk, pair), 1)
    row = lax.broadcasted_iota(jnp.int32, (blk, blk), 0)
    col = lax.broadcasted_iota(jnp.int32, (blk, blk), 1)
    uj = uj_ref[...]

    carry_ref[...] = jnp.zeros(carry_ref.shape, F32)
    acc_ref[...] = jnp.zeros(acc_ref.shape, F32)

    def sweep_block(j):
        start = pl.multiple_of(j * blk, blk)
        causal = (j * blk + col) < (i * blk + row)
        heads = range(2 * pairs)
        zs = []
        for p in range(pairs):
            ps = slice(p * pair, (p + 1) * pair)
            q = q_ref[:, ps]
            kt = kt_ref[ps, pl.ds(start, blk)]
            zero = jnp.zeros_like(q)
            zs.append(_dot(jnp.where(lane < head_dim, q, zero), kt))
            zs.append(_dot(jnp.where(lane >= head_dim, q, zero), kt))
        cums, log_betas = [], []
        for hd in heads:
            z = zs[hd]
            l1p = jnp.log(1.0 + jnp.exp(-jnp.abs(z)))
            log_keep = jnp.where(causal, -(jnp.maximum(z, 0.0) + l1p), 0.0)
            log_betas.append(jnp.minimum(z, 0.0) - l1p)
            cums.append(_dot(log_keep.astype(BF16), uj))
        top = None
        for hd in heads:
            p = hd // 2
            vb = v_ref[pl.ds(start, blk), p * pair:(p + 1) * pair]
            carry = carry_ref[hd]
            later = cums[hd][:, :blk] + carry
            w = jnp.where(causal, jnp.exp(log_betas[hd] + later), 0.0)
            acc_ref[hd] += _dot(w.astype(BF16), vb)
            carry = carry + cums[hd][:, blk:]
            carry_ref[hd] = carry
            m = jnp.max(carry, axis=0, keepdims=True)
            top = m if top is None else jnp.maximum(top, m)
        return jnp.max(top)

    top0 = sweep_block(i)

    def cond(state):
        j, top = state
        return jnp.logical_and(j >= 0, top > SB_DEAD_LOG)

    def body(state):
        j, _ = state
        return j - 1, sweep_block(j)

    lax.while_loop(cond, body, (i - 1, top0))
    for p in range(pairs):
        o_ref[:, p * pair:(p + 1) * pair] = jnp.where(lane < head_dim, acc_ref[2 * p], acc_ref[2 * p + 1]).astype(BF16)


def _stickbreak_attention(q, kt, v, head_dim):
    b, lp, width = q.shape
    blk = SEQ_ALIGN
    pair = 2 * head_dim
    assert pair == LANES and width % pair == 0 and lp % blk == 0
    pairs = width // pair
    r = jnp.arange(blk)
    later_mat = (r[:, None] > r[None, :]).astype(BF16)
    uj = jnp.concatenate([later_mat, jnp.ones((blk, blk), BF16)], axis=1)
    kern = functools.partial(_stickbreak_kernel, blk=blk, head_dim=head_dim, pairs=pairs)
    resident = pl.Buffered(1)
    return pl.pallas_call(
        kern,
        grid=(b, lp // blk),
        in_specs=[
            pl.BlockSpec((None, blk, width), lambda bi, i: (bi, i, 0)),
            pl.BlockSpec((None, width, lp), lambda bi, i: (bi, 0, 0), pipeline_mode=resident),
            pl.BlockSpec((None, lp, width), lambda bi, i: (bi, 0, 0), pipeline_mode=resident),
            pl.BlockSpec((blk, 2 * blk), lambda bi, i: (0, 0)),
        ],
        out_specs=pl.BlockSpec((None, blk, width), lambda bi, i: (bi, i, 0)),
        out_shape=jax.ShapeDtypeStruct((b, lp, width), BF16),
        scratch_shapes=[pltpu.VMEM((2 * pairs, blk, blk), F32), pltpu.VMEM((2 * pairs, blk, pair), F32)],
        compiler_params=_cparams(("arbitrary", "arbitrary")),
        name="stickbreak_attention",
    )(q, kt, v, uj)


def _even_out_kernel(h_ref, att_ref, conv_ref, wa_ref, wc_ref, o_ref):
    o_ref[...] = h_ref[...] + _dot(att_ref[...], wa_ref[...]) + _dot(conv_ref[...], wc_ref[...])


def _even_out(h2, att2, conv2, w_out, tile):
    n, d = h2.shape
    width = att2.shape[1]
    w = w_out.astype(BF16)
    return pl.pallas_call(
        _even_out_kernel,
        grid=(n // tile,),
        in_specs=[
            pl.BlockSpec((tile, d), lambda i: (i, 0)),
            pl.BlockSpec((tile, width), lambda i: (i, 0)),
            pl.BlockSpec((tile, width), lambda i: (i, 0)),
            pl.BlockSpec((width, d), lambda i: (0, 0)),
            pl.BlockSpec((width, d), lambda i: (0, 0)),
        ],
        out_specs=pl.BlockSpec((tile, d), lambda i: (i, 0)),
        out_shape=jax.ShapeDtypeStruct((n, d), F32),
        compiler_params=_cparams(("arbitrary",)),
        name="even_out",
    )(h2, att2, conv2, w[:width], w[width:])


def _odd_kernel(h_ref, g_ref, pw_ref, ps_ref, o_ref, buf, *, tile, group):
    i = pl.program_id(1)
    halo = max(POOL_WINDOWS)
    x = h_ref[...]
    hn = _rms_rows(x, g_ref[...])

    @pl.when(i == 0)
    def _():
        buf[0:halo, :] = jnp.zeros((halo, hn.shape[1]), F32)

    buf[halo:halo + tile, :] = hn
    pos = i * tile + lax.broadcasted_iota(jnp.int32, (tile, group), 0)
    ys = []
    for gi, w in enumerate(POOL_WINDOWS):
        c0 = gi * group
        cur = hn[:, c0:c0 + group]
        s = cur
        for sft in range(1, w):
            s = s + buf[halo - sft:halo - sft + tile, c0:c0 + group]
        count = jnp.minimum(pos + 1, w).astype(F32)
        diff = s / count - cur
        ys.append(_dot(diff.astype(BF16), pw_ref[gi]))
    y = jnp.concatenate(ys, axis=1)
    o_ref[...] = x + y * ps_ref[...]
    buf[0:halo, :] = hn[tile - halo:tile, :]


def _odd_layer(h, norm_g, pool_w, pool_scale, tile):
    b, lp, d = h.shape
    groups, group, _ = pool_w.shape
    assert groups == len(POOL_WINDOWS) and groups * group == d
    kern = functools.partial(_odd_kernel, tile=tile, group=group)
    return pl.pallas_call(
        kern,
        grid=(b, lp // tile),
        in_specs=[
            pl.BlockSpec((None, tile, d), lambda bi, i: (bi, i, 0)),
            pl.BlockSpec((1, d), lambda bi, i: (0, 0)),
            pl.BlockSpec((groups, group, group), lambda bi, i: (0, 0, 0)),
            pl.BlockSpec((1, d), lambda bi, i: (0, 0)),
        ],
        out_specs=pl.BlockSpec((None, tile, d), lambda bi, i: (bi, i, 0)),
        out_shape=jax.ShapeDtypeStruct((b, lp, d), F32),
        scratch_shapes=[pltpu.VMEM((tile + max(POOL_WINDOWS), d), F32)],
        compiler_params=_cparams(("arbitrary", "arbitrary")),
        name="odd_layer",
    )(h, norm_g.reshape(1, d), pool_w.astype(BF16), pool_scale.reshape(1, d))


def _fold_leading(x, op):
    parts = [x[i] for i in range(x.shape[0])]
    while len(parts) > 1:
        parts = [op(parts[i], parts[i + 1]) for i in range(0, len(parts) - 1, 2)] + parts[len(parts) & ~1:]
    return parts[0]


def _fold_rows(x, op):
    for shift in (4, 2, 1):
        x = op(x, pltpu.roll(x, shift, 0))
    return x


def _top16_rows(s, exact_ties, want_rank):
    groups = s.shape[0]
    if exact_ties:
        key_id = (lax.broadcasted_iota(jnp.int32, s.shape, 0) * SUBLANES
                  + lax.broadcasted_iota(jnp.int32, s.shape, 1)).astype(F32)
    rank = jnp.full(s.shape, float(PEER_TOPK), F32) if want_rank else None
    vals = []
    for it in range(PEER_TOPK):
        m = _fold_rows(_fold_leading(s, jnp.maximum), jnp.maximum)
        hit = s == m[None]
        if exact_ties:
            first = _fold_leading(jnp.where(hit, key_id, float(groups * SUBLANES)), jnp.minimum)
            hit = key_id == _fold_rows(first, jnp.minimum)[None]
        if want_rank:
            rank = jnp.where(hit, float(it), rank)
        s = jnp.where(hit, NEG_INF, s)
        vals.append(m)
    return rank, vals, s


def _pair_partner_counts(v1, v2, exact_ties):
    k = PEER_TOPK
    half = k // 2
    assert half == SUBLANES
    sub = lax.broadcasted_iota(jnp.int32, (SUBLANES, LANES), 0)

    def as_rows(vals):
        out = vals[0]
        for r in range(1, SUBLANES):
            out = jnp.where(sub == r, vals[r], out)
        return out

    v2_lo, v2_hi = as_rows(v2[:half]), as_rows(v2[half:])
    tiles = [v1[0] + v2_lo, v1[0] + v2_hi]
    flat_base = [0, half]
    masked = [0, 0]
    for a in range(1, half):
        bmax = k // (a + 1) - 1
        tiles.append(jnp.where(sub <= bmax, v1[a] + v2_lo, NEG_INF))
        flat_base.append(a * k)
        masked.append(half - 1 - bmax)
    tiles.append(as_rows(v1[half:]) + v2[0])
    if exact_ties:
        flats = [(sub + base).astype(F32) for base in flat_base] + [((sub + half) * k).astype(F32)]

    top = v1[0] + v2[0]
    denom = jnp.zeros((SUBLANES, LANES), F32)
    big = float(k * k)
    for _ in range(k):
        m = tiles[0]
        for t in tiles[1:]:
            m = jnp.maximum(m, t)
        m = _fold_rows(m, jnp.maximum)
        if exact_ties:
            first = None
            for t, f in zip(tiles, flats):
                c = jnp.where(t == m, f, big)
                first = c if first is None else jnp.minimum(first, c)
            first = _fold_rows(first, jnp.minimum)
            tiles = [jnp.where(f == first, NEG_INF, t) for t, f in zip(tiles, flats)]
        else:
            tiles = [jnp.where(t == m, NEG_INF, t) for t in tiles]
        denom = denom + jnp.exp(m - top)
    gone = [jnp.where(t == NEG_INF, 1.0, 0.0) for t in tiles]
    counts = [_fold_rows(gone[0] + gone[1], jnp.add)]
    counts += [_fold_rows(gone[a + 1], jnp.add) - float(masked[a + 1]) for a in range(1, half)]
    counts += [_fold_rows(jnp.where(sub == r, gone[-1], 0.0), jnp.add) for r in range(half)]
    return counts, denom


def _route_chunk(s1, s2, exact_ties):
    k = float(PEER_TOPK)
    rank1, v1, left1 = _top16_rows(s1, exact_ties, want_rank=exact_ties)
    rank2, v2, left2 = _top16_rows(s2, exact_ties, want_rank=True)
    counts, denom = _pair_partner_counts(v1, v2, exact_ties)
    n1 = jnp.zeros(s1.shape, F32)
    for a in range(PEER_TOPK):
        is_a = (rank1 == float(a)) if exact_ties else (s1 == v1[a][None])
        n1 = jnp.where(is_a, counts[a][None], n1)
    f1 = jnp.exp(s1 - v1[0][None])
    f2 = jnp.exp(s2 - v2[0][None]) / denom[None]
    if exact_ties:
        bad = jnp.zeros((SUBLANES, LANES), F32)
    else:
        def picked(left):
            return _fold_rows(_fold_leading(jnp.where(left == NEG_INF, 1.0, 0.0), jnp.add), jnp.add)

        pairs = counts[0]
        for c in counts[1:]:
            pairs = pairs + c
        ok = jnp.logical_and(jnp.logical_and(picked(left1) == k, picked(left2) == k), pairs == k)
        bad = jnp.where(ok, 0.0, 1.0)
    return n1, f1, rank2, f2, bad


def _peer_route_kernel(h_ref, g_ref, kwh_ref, kwl_ref,
                       hnt_ref, r2_ref, f2_ref, n1_ref, f1_ref, hi_scr, lo_scr, s_scr, *, tile, nkeys):
    hn = _rms_rows(h_ref[...], g_ref[...])
    hn_hi, hn_lo = _split_bf16(hn.T)
    hnt_ref[...] = hn_hi
    hi_scr[...] = hn_hi
    lo_scr[...] = hn_lo
    chunks = [slice(c * LANES, (c + 1) * LANES) for c in range(tile // LANES)]

    def pair_scores(hp, slot):
        rows = pl.ds(pl.multiple_of(hp * 4 * nkeys, 4 * nkeys), 4 * nkeys)
        kwh = kwh_ref[rows, :]
        hi = hi_scr[...]
        s_scr[slot] = _dot(kwh, hi) + _dot(kwh, lo_scr[...]) + _dot(kwl_ref[rows, :], hi)

    pair_scores(0, 0)

    def per_pair(hp, _):
        slot = lax.rem(hp, 2)
        s_all = s_scr[slot]
        pair_scores(jnp.minimum(hp + 1, PEER_HEADS // 2 - 1), 1 - slot)

        def keys_by_tile(x):
            return x.reshape(nkeys // SUBLANES, SUBLANES, LANES)

        def write(hh, cs, n1, f1, rank2, f2):
            n1_ref[hh, :, cs] = n1.reshape(nkeys, LANES)
            f1_ref[hh, :, cs] = f1.reshape(nkeys, LANES)
            r2_ref[hh, :, cs] = rank2.reshape(nkeys, LANES).astype(BF16)
            f2_ref[hh, :, cs] = f2.reshape(nkeys, LANES).astype(BF16)

        def route(j, cs, exact_ties):
            s1 = s_all[j * 2 * nkeys:j * 2 * nkeys + nkeys, cs]
            s2 = s_all[j * 2 * nkeys + nkeys:(j + 1) * 2 * nkeys, cs]
            return _route_chunk(keys_by_tile(s1), keys_by_tile(s2), exact_ties)

        bad = None
        for j in range(2):
            for cs in chunks:
                n1, f1, rank2, f2, bad_c = route(j, cs, exact_ties=False)
                write(2 * hp + j, cs, n1, f1, rank2, f2)
                bad = bad_c if bad is None else jnp.maximum(bad, bad_c)

        @pl.when(jnp.max(bad) > 0.0)
        def _():
            for j in range(2):
                for cs in chunks:
                    write(2 * hp + j, cs, *route(j, cs, exact_ties=True)[:4])
        return 0

    lax.fori_loop(0, PEER_HEADS // 2, per_pair, 0)


def _split3_bf16(x):
    hi = x.astype(BF16)
    r1 = x - hi.astype(F32)
    mid = r1.astype(BF16)
    lo = (r1 - mid.astype(F32)).astype(BF16)
    return hi, mid, lo


def _fold_keys_kernel(k_ref, wq_ref, o_ref):
    kh, km, kl = _split3_bf16(k_ref[...])
    wh, wm, wl = _split3_bf16(wq_ref[...])
    o_ref[...] = (_dot(kh, wh) + (_dot(kh, wm) + _dot(km, wh))
                  + (_dot(km, wm) + _dot(kh, wl) + _dot(kl, wh)))


def _fold_keys(w_query, sub_keys):
    d = w_query.shape[0]
    _, heads, nkeys, half = sub_keys.shape
    keys = sub_keys.transpose(1, 0, 2, 3).reshape(heads * 2, nkeys, half)
    return pl.pallas_call(
        _fold_keys_kernel,
        grid=(heads * 2,),
        in_specs=[pl.BlockSpec((None, nkeys, half), lambda r: (r, 0, 0)),
                  pl.BlockSpec((half, d), lambda r: (r, 0))],
        out_specs=pl.BlockSpec((nkeys, d), lambda r: (r, 0)),
        out_shape=jax.ShapeDtypeStruct((heads * 2 * nkeys, d), F32),
        compiler_params=_cparams(("arbitrary",)),
        name="peer_fold_keys",
    )(keys, w_query.T)


def _peer_route(h2, norm_g, w_query, sub_keys, tile):
    n, d = h2.shape
    _, heads, nkeys, half = sub_keys.shape
    assert heads == PEER_HEADS and nkeys == half == LANES
    srows = heads * 2 * nkeys
    kw_hi, kw_lo = _split_bf16(_fold_keys(w_query, sub_keys))
    kern = functools.partial(_peer_route_kernel, tile=tile, nkeys=nkeys)
    tab = lambda dt: jax.ShapeDtypeStruct((heads, nkeys, n), dt)
    tab_spec = pl.BlockSpec((heads, nkeys, tile), lambda i: (0, 0, i))
    const = lambda shape: pl.BlockSpec(shape, lambda i: (0,) * len(shape))
    return pl.pallas_call(
        kern,
        grid=(n // tile,),
        in_specs=[
            pl.BlockSpec((tile, d), lambda i: (i, 0)),
            const((1, d)), const((srows, d)), const((srows, d)),
        ],
        out_specs=(pl.BlockSpec((d, tile), lambda i: (0, i)), tab_spec, tab_spec, tab_spec, tab_spec),
        out_shape=(jax.ShapeDtypeStruct((d, n), BF16), tab(BF16), tab(BF16), tab(F32), tab(F32)),
        scratch_shapes=[pltpu.VMEM((d, tile), BF16), pltpu.VMEM((d, tile), BF16),
                        pltpu.VMEM((2, 4 * nkeys, tile), F32)],
        compiler_params=_cparams(("arbitrary",)),
        name="peer_route",
    )(h2, norm_g.reshape(1, d), kw_hi, kw_lo)


def _gelu_times_gate(a, gate):
    k = -2.0 * math.sqrt(2.0 / math.pi) * math.log2(math.e)
    u = a * ((a * a) * (k * 0.044715) + k)
    return a.astype(BF16) * gate / (1.0 + jnp.exp2(u.astype(BF16)))


def _pack_rows_kernel(x_ref, o_ref):
    o_ref[...] = pltpu.bitcast(x_ref[...].astype(BF16), jnp.uint32)


def _pack_cols_kernel(x_ref, o_ref):
    o_ref[...] = pltpu.bitcast(x_ref[...].T.astype(BF16), jnp.uint32)


def _pack_expert_weights(expert_u, expert_v, layer, rows):
    _, n_exp, d = expert_u.shape
    grid = (n_exp // rows,)
    in_spec = pl.BlockSpec((None, rows, d), lambda i: (layer, i, 0))
    u = pl.pallas_call(
        _pack_rows_kernel, grid=grid, in_specs=[in_spec],
        out_specs=pl.BlockSpec((rows // 2, d), lambda i: (i, 0)),
        out_shape=jax.ShapeDtypeStruct((n_exp // 2, d), jnp.uint32),
        compiler_params=_cparams(("arbitrary",)), name="peer_pack_u",
    )(expert_u)
    vt = pl.pallas_call(
        _pack_cols_kernel, grid=grid, in_specs=[in_spec],
        out_specs=pl.BlockSpec((d // 2, rows), lambda i: (0, i)),
        out_shape=jax.ShapeDtypeStruct((d // 2, n_exp), jnp.uint32),
        compiler_params=_cparams(("arbitrary",)), name="peer_pack_vt",
    )(expert_v)
    return u, vt


def _peer_expert_kernel(hnt_ref, u_ref, vt_ref, r2_ref, f2_ref, n1_ref, f1_ref, h_ref,
                        o_ref, acc_ref, ga_ref, *, tile, nkeys, rows_per_step):
    e = pl.program_id(1)

    @pl.when(e == 0)
    def _():
        acc_ref[...] = jnp.zeros(acc_ref.shape, F32)

    pre = _dot(pltpu.bitcast(u_ref[...], BF16), hnt_ref[...])
    sub = 16
    for r in range(rows_per_step):
        bcast = []
        for hh in range(PEER_HEADS):
            n1 = jnp.broadcast_to(n1_ref[hh, r:r + 1, :], (sub, tile)).astype(BF16)
            f1 = jnp.broadcast_to(f1_ref[hh, r:r + 1, :], (sub, tile)).astype(BF16)
            bcast.append((n1, f1))
        for g in range(nkeys // sub):
            rs = slice(g * sub, (g + 1) * sub)
            gate = jnp.zeros((sub, tile), BF16)
            for hh in range(PEER_HEADS):
                n1, f1 = bcast[hh]
                gate = gate + jnp.where(r2_ref[hh, rs, :] < n1, f2_ref[hh, rs, :], jnp.zeros((), BF16)) * f1
            o0 = r * nkeys + g * sub
            ga_ref[o0:o0 + sub, :] = _gelu_times_gate(pre[o0:o0 + sub, :], gate)
    acc_ref[...] += _dot(pltpu.bitcast(vt_ref[...], BF16), ga_ref[...])

    @pl.when(e == pl.num_programs(1) - 1)
    def _():
        o_ref[...] = h_ref[...] + acc_ref[...].T


def _peer_experts(h2, hnt, r2, f2, n1, f1, expert_u, expert_v, layer, tile, rows_per_step):
    n, d = h2.shape
    heads, nkeys, _ = r2.shape
    n_exp = expert_u.shape[1]
    te = rows_per_step * nkeys
    assert n_exp == nkeys * nkeys and n_exp % te == 0 and rows_per_step % 8 == 0
    u, vt = _pack_expert_weights(expert_u, expert_v, layer, rows=512)
    kern = functools.partial(_peer_expert_kernel, tile=tile, nkeys=nkeys, rows_per_step=rows_per_step)
    tab2 = pl.BlockSpec((heads, nkeys, tile), lambda t, e: (0, 0, t))
    tab1 = pl.BlockSpec((heads, rows_per_step, tile), lambda t, e: (0, e, t))
    return pl.pallas_call(
        kern,
        grid=(n // tile, n_exp // te),
        in_specs=[
            pl.BlockSpec((d, tile), lambda t, e: (0, t)),
            pl.BlockSpec((te // 2, d), lambda t, e: (e, 0)),
            pl.BlockSpec((d // 2, te), lambda t, e: (0, e)),
            tab2, tab2, tab1, tab1,
            pl.BlockSpec((tile, d), lambda t, e: (t, 0)),
        ],
        out_specs=pl.BlockSpec((tile, d), lambda t, e: (t, 0)),
        out_shape=jax.ShapeDtypeStruct((n, d), F32),
        scratch_shapes=[pltpu.VMEM((d, tile), F32), pltpu.VMEM((te, tile), BF16)],
        compiler_params=_cparams(("arbitrary", "arbitrary")),
        name="peer_experts",
    )(hnt, u, vt, r2, f2, n1, f1, h2)


def _peer_ffn(h2, norm_g, w_query, sub_keys, expert_u, expert_v, layer, route_tile, expert_tile):
    hnt, r2, f2, n1, f1 = _peer_route(h2, norm_g, w_query, sub_keys, route_tile)
    return _peer_experts(h2, hnt, r2, f2, n1, f1, expert_u, expert_v, layer, expert_tile, rows_per_step=16)


def _pick_tile(n, candidates):
    for c in candidates:
        if n % c == 0:
            return c
    raise ValueError(f"no tile in {candidates} divides {n}")


def kernel(x, meta_tokens, even_norm_g, even_w_in, even_q_norm_g, even_k_norm_g, even_conv_w, even_w_out,
           odd_norm_g, odd_pool_w, odd_pool_scale, ffn_norm_g, peer_w_query, peer_sub_keys, peer_u, peer_v):
    b, seq, d = x.shape
    total = N_META + seq
    lp = -(-total // SEQ_ALIGN) * SEQ_ALIGN
    meta = jnp.broadcast_to(meta_tokens[None].astype(x.dtype), (b, N_META, d))
    h = jnp.concatenate([meta, x], axis=1)
    h = jnp.pad(h, ((0, 0), (0, lp - total), (0, 0)))
    n = b * lp
    seq_tile = _pick_tile(lp, (384, 256, 128))
    flat_tile = _pick_tile(n, (768, 512, 384, 256, 128))
    route_tile = _pick_tile(n, (256, 128))
    depth = ffn_norm_g.shape[0]
    head_dim = even_q_norm_g.shape[1]
    for layer in range(depth):
        i = layer // 2
        if layer % 2 == 0:
            q, kt, v, conv = _even_front(h, even_norm_g[i], even_w_in[i], even_q_norm_g[i], even_k_norm_g[i],
                                         even_conv_w[i], seq_tile)
            att = _stickbreak_attention(q, kt, v, head_dim)
            width = att.shape[-1]
            h2 = _even_out(h.reshape(n, d), att.reshape(n, width), conv.reshape(n, width), even_w_out[i], flat_tile)
        else:
            h2 = _odd_layer(h, odd_norm_g[i], odd_pool_w[i], odd_pool_scale[i], seq_tile).reshape(n, d)
        h2 = _peer_ffn(h2, ffn_norm_g[layer], peer_w_query[layer], peer_sub_keys[layer], peer_u, peer_v, layer,
                       route_tile, flat_tile)
        h = h2.reshape(b, lp, d)
    return h[:, N_META:N_META + seq]
```
